```python
import math
import jax
import jax.numpy as jnp
from jax import lax
import numpy as np


D_MODEL = 1024
BATCH = 8
SEQ = 8192
DEPTH = 4

GRID_W = 64
CTX_LEN = 256
N_MIXERS = 3
NORM_EPS = 1e-6

ATTN_HEADS = 8
ATTN_KV_HEADS = 2
ATTN_HEAD_DIM = D_MODEL // ATTN_HEADS
ATTN_GROUP = ATTN_HEADS // ATTN_KV_HEADS
ROPE_THETA = 10000.0
Q_BLOCK = 128

S5_GROUP = 16
S5_GROUPS = D_MODEL // S5_GROUP
S5_STATE = 64
S5_CHUNK = 128
S5_DT_MIN = 1e-3
S5_DT_MAX = 1e-1

SSD_INNER = 2 * D_MODEL
SSD_HEAD_DIM = 64
SSD_HEADS = SSD_INNER // SSD_HEAD_DIM
SSD_GROUPS = 4
SSD_HEADS_PER_GROUP = SSD_HEADS // SSD_GROUPS
SSD_STATE = 128
SSD_CONV = 5
SSD_CHUNK = 128
SSD_CONV_CH = SSD_INNER + 2 * SSD_GROUPS * SSD_STATE
SSD_PROJ = SSD_INNER + SSD_CONV_CH + 2 * SSD_HEADS
SSD_DT_MIN = 1e-3
SSD_DT_MAX = 1e-1

MOE_GROUPS = 4
MOE_EXPERTS_PER_GROUP = 8
MOE_EXPERTS = MOE_GROUPS * MOE_EXPERTS_PER_GROUP
MOE_TOP_K = 2
MOE_HIDDEN = 256

kernel_name = 'hybrid_gqa_s5_ssd_hmoe_prefix_dit'


def rmsnorm(h, g):
    h32 = h.astype(jnp.float32)
    h32 = h32 * lax.rsqrt(jnp.mean(h32 * h32, axis=-1, keepdims=True) + NORM_EPS)
    return h32.astype(h.dtype) * g


def modulate(h, shift, scale):
    return h * (1.0 + scale) + shift


def axial_rope_tables(rows, dtype):
    row, col = jnp.meshgrid(jnp.arange(rows), jnp.arange(GRID_W), indexing='ij')
    n_tokens = rows * GRID_W
    pos = jnp.stack([row.reshape(-1), col.reshape(-1)], axis=-1).astype(jnp.float32)
    n_freq = ATTN_HEAD_DIM // 4
    inv_freq = ROPE_THETA ** (-jnp.arange(n_freq, dtype=jnp.float32) / n_freq)
    ang = jnp.broadcast_to(pos[:, :, None, None] * inv_freq, (n_tokens, 2, 2, n_freq))
    ang = ang.reshape(n_tokens, ATTN_HEAD_DIM)
    return jnp.cos(ang).astype(dtype), jnp.sin(ang).astype(dtype)


def axial_rope(x, cos, sin):
    xs = x.reshape(x.shape[:-1] + (2, 2, ATTN_HEAD_DIM // 4))
    rot = jnp.stack([-xs[..., 1, :], xs[..., 0, :]], axis=-2).reshape(x.shape)
    return x * cos[:, None, :] + rot * sin[:, None, :]


def gqa_axial_attention(u_lat, u_ctx, cos, sin, w_qkv, q_gain, k_gain, w_o, need_ctx):
    bn, s_len, _ = u_lat.shape
    hd = ATTN_HEAD_DIM

    def project(u):
        n = u.shape[1]
        q, k, v = jnp.split(u @ w_qkv, [ATTN_HEADS * hd, (ATTN_HEADS + ATTN_KV_HEADS) * hd], axis=-1)
        q = rmsnorm(q.reshape(bn, n, ATTN_HEADS, hd), q_gain)
        k = rmsnorm(k.reshape(bn, n, ATTN_KV_HEADS, hd), k_gain)
        v = v.reshape(bn, n, ATTN_KV_HEADS, hd)
        return q, k, v

    def attend(q, k, v):
        sc = jnp.einsum('bqkgd,blkd->bkgql', q, k).astype(jnp.float32) * (hd ** -0.5)
        p = jax.nn.softmax(sc, axis=-1).astype(v.dtype)
        return jnp.einsum('bkgql,blkd->bqkgd', p, v)

    q_l, k_l, v_l = project(u_lat)
    q_l = axial_rope(q_l, cos, sin)
    k_l = axial_rope(k_l, cos, sin)
    q_c, k_c, v_c = project(u_ctx)
    k_all = jnp.concatenate([k_c, k_l], axis=1)
    v_all = jnp.concatenate([v_c, v_l], axis=1)
    nb = s_len // Q_BLOCK
    q_blocks = q_l.reshape(bn, nb, Q_BLOCK, ATTN_KV_HEADS, ATTN_GROUP, hd).swapaxes(0, 1)
    o = lax.map(lambda qb: attend(qb, k_all, v_all), q_blocks)
    o = o.swapaxes(0, 1).reshape(bn, s_len, ATTN_HEADS * hd)
    y_lat = o @ w_o
    y_ctx = None
    if need_ctx:
        n_c = u_ctx.shape[1]
        oc = attend(q_c.reshape(bn, n_c, ATTN_KV_HEADS, ATTN_GROUP, hd), k_c, v_c)
        y_ctx = oc.reshape(bn, n_c, ATTN_HEADS * hd) @ w_o
    return y_lat, y_ctx


def s5_discretise(a_re, a_im, log_dt, b_re, b_im):
    dt = jnp.exp(log_dt)[:, None]
    mag = jnp.exp(a_re * dt)
    ab_re = mag * jnp.cos(a_im * dt)
    ab_im = mag * jnp.sin(a_im * dt)
    den = a_re * a_re + a_im * a_im
    num_re = ab_re - 1.0
    coef_re = (num_re * a_re + ab_im * a_im) / den
    coef_im = (ab_im * a_re - num_re * a_im) / den
    bb_re = coef_re[..., None] * b_re - coef_im[..., None] * b_im
    bb_im = coef_re[..., None] * b_im + coef_im[..., None] * b_re
    return ab_re, ab_im, bb_re, bb_im


def s5_chunk_scan(u, ab_re, ab_im, bb_re, bb_im, c_re, c_im, s_re, s_im):
    bn, n = u.shape[:2]
    uc = u.reshape(bn, n // S5_CHUNK, S5_CHUNK, S5_GROUPS, S5_GROUP).swapaxes(0, 1)

    def combine(e1, e2):
        a1r, a1i, b1r, b1i = e1
        a2r, a2i, b2r, b2i = e2
        return (a2r * a1r - a2i * a1i, a2r * a1i + a2i * a1r,
                a2r * b1r - a2i * b1i + b2r, a2r * b1i + a2i * b1r + b2i)

    def step(carry, ub):
        sr, si = carry
        bu_re = jnp.einsum('bqgc,gpc->bqgp', ub, bb_re)
        bu_im = jnp.einsum('bqgc,gpc->bqgp', ub, bb_im)
        bu_re = bu_re.at[:, 0].add(ab_re * sr - ab_im * si)
        bu_im = bu_im.at[:, 0].add(ab_re * si + ab_im * sr)
        ar = jnp.broadcast_to(ab_re, bu_re.shape)
        ai = jnp.broadcast_to(ab_im, bu_im.shape)
        _, _, xr, xi = lax.associative_scan(combine, (ar, ai, bu_re, bu_im), axis=1)
        y = jnp.einsum('bqgp,gcp->bqgc', xr, c_re) - jnp.einsum('bqgp,gcp->bqgc', xi, c_im)
        return (xr[:, -1], xi[:, -1]), y

    (sr, si), ys = lax.scan(step, (s_re, s_im), uc)
    return ys.swapaxes(0, 1).reshape(u.shape), sr, si


def s5_mixer(u_lat, u_ctx, a_re, a_im, log_dt, b_re, b_im, c_re, c_im, d_skip, w_glu, b_glu, need_ctx):
    f32 = jnp.float32
    bn = u_lat.shape[0]

    def groups(u):
        return u.astype(f32).reshape(u.shape[0], u.shape[1], S5_GROUPS, S5_GROUP)

    def flip(t):
        return jnp.flip(t, axis=1)

    g_lat, g_ctx = groups(u_lat), groups(u_ctx)
    dsk = d_skip.astype(f32).reshape(S5_GROUPS, S5_GROUP)
    y_lat = g_lat * dsk
    y_ctx = g_ctx * dsk
    s0 = jnp.zeros((bn, S5_GROUPS, S5_STATE), f32)
    for dr in range(2):
        disc = s5_discretise(a_re[dr].astype(f32), a_im[dr].astype(f32), log_dt[dr].astype(f32),
                             b_re[dr].astype(f32), b_im[dr].astype(f32))
        cr, ci = c_re[dr].astype(f32), c_im[dr].astype(f32)
        seq_c = g_ctx if dr == 0 else flip(g_ctx)
        seq_l = g_lat if dr == 0 else flip(g_lat)
        yc, s_re, s_im = s5_chunk_scan(seq_c, *disc, cr, ci, s0, s0)
        yl, _, _ = s5_chunk_scan(seq_l, *disc, cr, ci, s_re, s_im)
        if dr == 1:
            yc, yl = flip(yc), flip(yl)
        y_lat = y_lat + yl
        y_ctx = y_ctx + yc

    def glu(y, ref):
        z = jax.nn.gelu(y.reshape(ref.shape).astype(ref.dtype))
        a, b = jnp.split(z @ w_glu + b_glu, 2, axis=-1)
        return a * jax.nn.sigmoid(b)

    return glu(y_lat, u_lat), (glu(y_ctx, u_ctx) if need_ctx else None)


def centred_depthwise_conv(h, w, b):
    out = lax.conv_general_dilated(h, w[:, None, :].astype(h.dtype), window_strides=(1,),
                                   padding=[(SSD_CONV // 2, SSD_CONV // 2)],
                                   dimension_numbers=('NWC', 'WIO', 'NWC'),
                                   feature_group_count=h.shape[-1])
    return out + b


def ssd_chunk_scan(xs, dt, a, bm, cm, s0):
    bn, n = xs.shape[:2]
    nc = n // SSD_CHUNK

    def chunks(t):
        return t.reshape((bn, nc, SSD_CHUNK) + t.shape[2:]).swapaxes(0, 1)

    lower = jnp.tril(jnp.ones((SSD_CHUNK, SSD_CHUNK), dtype=bool))

    def step(s, blk):
        xq, dtq, bq, cq = blk
        cum = jnp.cumsum(dtq * a, axis=1)
        seg = cum[:, :, None] - cum[:, None, :]
        decay = jnp.exp(jnp.where(lower[None, :, :, None, None], seg, -jnp.inf))
        xdt = xq * dtq[..., None]
        cb = jnp.einsum('bign,bjgn->bijg', cq, bq)
        y = jnp.einsum('bijg,bijgh,bjghp->bighp', cb, decay, xdt)
        y = y + jnp.einsum('bign,bghpn,bigh->bighp', cq, s, jnp.exp(cum))
        tail = jnp.exp(cum[:, -1:] - cum)
        s = jnp.exp(cum[:, -1])[..., None, None] * s + jnp.einsum('bjgn,bjgh,bjghp->bghpn', bq, tail, xdt)
        return s, y

    s_fin, ys = lax.scan(step, s0, (chunks(xs), chunks(dt), chunks(bm), chunks(cm)))
    return ys.swapaxes(0, 1).reshape(xs.shape), s_fin


def ssd_mixer(u_lat, u_ctx, w_in, conv_w, conv_b, dt_bias, a_log, d_skip, norm_g, w_out, need_ctx):
    f32 = jnp.float32
    bn = u_lat.shape[0]
    G, Hg = SSD_GROUPS, SSD_HEADS_PER_GROUP

    def prep(u):
        n = u.shape[1]
        z, xbc, dt_raw = jnp.split(u @ w_in, [SSD_INNER, SSD_INNER + SSD_CONV_CH], axis=-1)
        xbc = jax.nn.silu(centred_depthwise_conv(xbc, conv_w, conv_b))
        xs, bm, cm = jnp.split(xbc, [SSD_INNER, SSD_INNER + G * SSD_STATE], axis=-1)
        xs = xs.reshape(bn, n, G, Hg, SSD_HEAD_DIM).astype(f32)
        bm = bm.reshape(bn, n, G, SSD_STATE).astype(f32)
        cm = cm.reshape(bn, n, G, SSD_STATE).astype(f32)
        dt = jax.nn.softplus(dt_raw.astype(f32).reshape(bn, n, 2, G, Hg) + dt_bias.astype(f32).reshape(2, G, Hg))
        return z, xs, bm, cm, dt

    def flip(t):
        return jnp.flip(t, axis=1)

    z_l, x_l, b_l, c_l, dt_l = prep(u_lat)
    z_c, x_c, b_c, c_c, dt_c = prep(u_ctx)
    a = -jnp.exp(a_log.astype(f32)).reshape(2, G, Hg)
    s0 = jnp.zeros((bn, G, Hg, SSD_HEAD_DIM, SSD_STATE), f32)
    yc_f, sc_f = ssd_chunk_scan(x_c, dt_c[:, :, 0], a[0], b_c, c_c, s0)
    yl_f, _ = ssd_chunk_scan(x_l, dt_l[:, :, 0], a[0], b_l, c_l, sc_f)
    yc_b, sc_b = ssd_chunk_scan(flip(x_c), flip(dt_c[:, :, 1]), a[1], flip(b_c), flip(c_c), s0)
    yl_b, _ = ssd_chunk_scan(flip(x_l), flip(dt_l[:, :, 1]), a[1], flip(b_l), flip(c_l), sc_b)
    dsk = d_skip.astype(f32).reshape(G, Hg, 1)

    def finish(y_f, y_b, xs, z):
        y = (y_f + flip(y_b) + xs * dsk).reshape(z.shape).astype(z.dtype)
        return rmsnorm(y * jax.nn.silu(z), norm_g) @ w_out

    y_lat = finish(yl_f, yl_b, x_l, z_l)
    y_ctx = finish(yc_f, yc_b, x_c, z_c) if need_ctx else None
    return y_lat, y_ctx


def hierarchical_moe(h, w_group, b_group, w_router, b_router, w_gate, w_up, w_down):
    shp = h.shape
    t = h.reshape(-1, D_MODEL)
    g_logits = (t @ w_group + b_group).astype(jnp.float32)
    g_idx = jnp.argmax(g_logits, axis=-1)
    g_w = jnp.max(jax.nn.softmax(g_logits, axis=-1), axis=-1, keepdims=True)
    e_logits = (t @ w_router + b_router).astype(jnp.float32).reshape(-1, MOE_GROUPS, MOE_EXPERTS_PER_GROUP)
    e_logits = jnp.einsum('tg,tge->te', jax.nn.one_hot(g_idx, MOE_GROUPS, dtype=jnp.float32), e_logits)
    e_w, e_idx = lax.top_k(jax.nn.softmax(e_logits, axis=-1), MOE_TOP_K)
    e_w = e_w / jnp.sum(e_w, axis=-1, keepdims=True)
    expert = g_idx[:, None] * MOE_EXPERTS_PER_GROUP + e_idx
    combine = jnp.einsum('tk,tke->te', g_w * e_w,
                         jax.nn.one_hot(expert, MOE_EXPERTS, dtype=jnp.float32)).astype(t.dtype)
    y = jnp.zeros_like(t)
    for e in range(MOE_EXPERTS):
        hid = jax.nn.silu(t @ w_gate[e]) * (t @ w_up[e])
        y = y + combine[:, e:e + 1] * (hid @ w_down[e])
    return y.reshape(shp)


def setup_inputs(seed: int = 0) -> dict:
    key = jax.random.key(seed)
    keys = iter(jax.random.split(key, 48))
    f32 = jnp.float32
    n_attn = len(range(0, DEPTH, N_MIXERS))
    n_s5 = len(range(1, DEPTH, N_MIXERS))
    n_ssd = len(range(2, DEPTH, N_MIXERS))

    def normal(shape, scale=1.0):
        return jax.random.normal(next(keys), shape, f32) * scale

    def uniform(shape, lo, hi):
        return jax.random.uniform(next(keys), shape, f32, lo, hi)

    def gain(shape):
        return 1.0 + normal(shape, 0.05)

    qkv_cols = (ATTN_HEADS + 2 * ATTN_KV_HEADS) * ATTN_HEAD_DIM
    ssd_dt = jnp.exp(uniform((n_ssd, 2, SSD_HEADS), math.log(SSD_DT_MIN), math.log(SSD_DT_MAX)))
    return {
        'x': normal((BATCH, SEQ, D_MODEL)),
        'c': normal((BATCH, D_MODEL)),
        'ctx': normal((BATCH, CTX_LEN, D_MODEL)),
        'c_ctx': normal((D_MODEL,)),
        'mod_w': normal((DEPTH, D_MODEL, 6 * D_MODEL), 0.5 * D_MODEL ** -0.5),
        'mod_b': normal((DEPTH, 6 * D_MODEL), 0.02),
        'norm1_g': gain((DEPTH, D_MODEL)),
        'norm2_g': gain((DEPTH, D_MODEL)),
        'attn_w_qkv': normal((n_attn, D_MODEL, qkv_cols), D_MODEL ** -0.5),
        'attn_q_gain': gain((n_attn, ATTN_HEAD_DIM)),
        'attn_k_gain': gain((n_attn, ATTN_HEAD_DIM)),
        'attn_w_o': normal((n_attn, ATTN_HEADS * ATTN_HEAD_DIM, D_MODEL), (ATTN_HEADS * ATTN_HEAD_DIM) ** -0.5),
        's5_a_re': -0.5 + normal((n_s5, 2, S5_GROUPS, S5_STATE), 0.01),
        's5_a_im': math.pi * jnp.arange(S5_STATE, dtype=f32) + normal((n_s5, 2, S5_GROUPS, S5_STATE), 0.01),
        's5_log_dt': uniform((n_s5, 2, S5_GROUPS), math.log(S5_DT_MIN), math.log(S5_DT_MAX)),
        's5_b_re': normal((n_s5, 2, S5_GROUPS, S5_STATE, S5_GROUP), (2 * S5_GROUP) ** -0.5),
        's5_b_im': normal((n_s5, 2, S5_GROUPS, S5_STATE, S5_GROUP), (2 * S5_GROUP) ** -0.5),
        's5_c_re': normal((n_s5, 2, S5_GROUPS, S5_GROUP, S5_STATE), S5_STATE ** -0.5),
        's5_c_im': normal((n_s5, 2, S5_GROUPS, S5_GROUP, S5_STATE), S5_STATE ** -0.5),
        's5_d': normal((n_s5, D_MODEL)),
        's5_w_glu': normal((n_s5, D_MODEL, 2 * D_MODEL), D_MODEL ** -0.5),
        's5_b_glu': normal((n_s5, 2 * D_MODEL), 0.02),
        'ssd_w_in': normal((n_ssd, D_MODEL, SSD_PROJ), D_MODEL ** -0.5),
        'ssd_conv_w': normal((n_ssd, SSD_CONV, SSD_CONV_CH), SSD_CONV ** -0.5),
        'ssd_conv_b': normal((n_ssd, SSD_CONV_CH), 0.02),
        'ssd_dt_bias': ssd_dt + jnp.log(-jnp.expm1(-ssd_dt)),
        'ssd_a_log': jnp.log(uniform((n_ssd, 2, SSD_HEADS), 1.0, 16.0)),
        'ssd_d': gain((n_ssd, SSD_HEADS)),
        'ssd_norm_g': gain((n_ssd, SSD_INNER)),
        'ssd_w_out': normal((n_ssd, SSD_INNER, D_MODEL), SSD_INNER ** -0.5),
        'moe_w_group': normal((DEPTH, D_MODEL, MOE_GROUPS), D_MODEL ** -0.5),
        'moe_b_group': normal((DEPTH, MOE_GROUPS), 0.01),
        'moe_w_router': normal((DEPTH, D_MODEL, MOE_EXPERTS), D_MODEL ** -0.5),
        'moe_b_router': normal((DEPTH, MOE_EXPERTS), 0.01),
        'moe_w_gate': normal((DEPTH, MOE_EXPERTS, D_MODEL, MOE_HIDDEN), D_MODEL ** -0.5),
        'moe_w_up': normal((DEPTH, MOE_EXPERTS, D_MODEL, MOE_HIDDEN), D_MODEL ** -0.5),
        'moe_w_down': normal((DEPTH, MOE_EXPERTS, MOE_HIDDEN, D_MODEL), MOE_HIDDEN ** -0.5),
    }


def reference(x, c, ctx, c_ctx, mod_w, mod_b, norm1_g, norm2_g,
              attn_w_qkv, attn_q_gain, attn_k_gain, attn_w_o,
              s5_a_re, s5_a_im, s5_log_dt, s5_b_re, s5_b_im, s5_c_re, s5_c_im, s5_d, s5_w_glu, s5_b_glu,
              ssd_w_in, ssd_conv_w, ssd_conv_b, ssd_dt_bias, ssd_a_log, ssd_d, ssd_norm_g, ssd_w_out,
              moe_w_group, moe_b_group, moe_w_router, moe_b_router, moe_w_gate, moe_w_up, moe_w_down):
    rows = x.shape[1] // GRID_W
    cos, sin = axial_rope_tables(rows, x.dtype)
    h_lat, h_ctx = x, ctx
    for i in range(DEPTH):
        need_ctx = i < DEPTH - 1
        m_l = jnp.split((jax.nn.silu(c) @ mod_w[i] + mod_b[i])[:, None, :], 6, axis=-1)
        m_c = jnp.split(jax.nn.silu(c_ctx) @ mod_w[i] + mod_b[i], 6, axis=-1)
        u_lat = modulate(rmsnorm(h_lat, norm1_g[i]), m_l[0], m_l[1])
        u_ctx = modulate(rmsnorm(h_ctx, norm1_g[i]), m_c[0], m_c[1])
        kind, j = i % N_MIXERS, i // N_MIXERS
        if kind == 0:
            y_lat, y_ctx = gqa_axial_attention(u_lat, u_ctx, cos, sin, attn_w_qkv[j], attn_q_gain[j],
                                               attn_k_gain[j], attn_w_o[j], need_ctx)
        elif kind == 1:
            y_lat, y_ctx = s5_mixer(u_lat, u_ctx, s5_a_re[j], s5_a_im[j], s5_log_dt[j], s5_b_re[j], s5_b_im[j],
                                    s5_c_re[j], s5_c_im[j], s5_d[j], s5_w_glu[j], s5_b_glu[j], need_ctx)
        else:
            y_lat, y_ctx = ssd_mixer(u_lat, u_ctx, ssd_w_in[j], ssd_conv_w[j], ssd_conv_b[j], ssd_dt_bias[j],
                                     ssd_a_log[j], ssd_d[j], ssd_norm_g[j], ssd_w_out[j], need_ctx)
        moe_args = (moe_w_group[i], moe_b_group[i], moe_w_router[i], moe_b_router[i],
                    moe_w_gate[i], moe_w_up[i], moe_w_down[i])
        h_lat = h_lat + m_l[2] * y_lat
        h_lat = h_lat + m_l[5] * hierarchical_moe(modulate(rmsnorm(h_lat, norm2_g[i]), m_l[3], m_l[4]), *moe_args)
        if need_ctx:
            h_ctx = h_ctx + m_c[2] * y_ctx
            h_ctx = h_ctx + m_c[5] * hierarchical_moe(modulate(rmsnorm(h_ctx, norm2_g[i]), m_c[3], m_c[4]), *moe_args)
    return h_lat
```

```python
import functools
import math

import jax
import jax.numpy as jnp
from jax import lax
from jax.experimental import pallas as pl
from jax.experimental.pallas import tpu as pltpu

F32 = jnp.float32
BF16 = jnp.bfloat16

NORM_EPS = 1e-6
GRID_W = 64
N_MIXERS = 3

ATTN_HEADS = 8
ATTN_KV_HEADS = 2
ATTN_GROUP = ATTN_HEADS // ATTN_KV_HEADS
HEAD_DIM = 128
ROPE_THETA = 10000.0

S5_GROUP = 16
S5_STATE = 64
S5_SLAB_GROUPS = 8
S5_SLAB_STATE = S5_SLAB_GROUPS * S5_STATE
S5_STEPS = 64

SSD_HEAD_DIM = 64
SSD_GROUPS = 4
SSD_STATE = 128
SSD_CONV = 5
SSD_CHUNK = 128

MOE_GROUPS = 4
MOE_PER_GROUP = 8
MOE_EXPERTS = MOE_GROUPS * MOE_PER_GROUP
MOE_HIDDEN = 256

LANES = 128
ROW_TILE = 256
VMEM_LIMIT = 56 * 1024 * 1024


def _params(*sem):
    return pltpu.CompilerParams(dimension_semantics=sem, vmem_limit_bytes=VMEM_LIMIT)


def _rms(x):
    return x * lax.rsqrt(jnp.mean(x * x, axis=-1, keepdims=True) + NORM_EPS)


def _silu(x):
    return x * (1.0 / (1.0 + jnp.exp(-x)))


def _sigmoid(x):
    return 1.0 / (1.0 + jnp.exp(-x))


def _dot(a, b):
    return jnp.dot(a, b, preferred_element_type=F32)


def _split_bf16(x):
    hi = x.astype(BF16)
    lo = (x - hi.astype(F32)).astype(BF16)
    return hi, lo


class _Rows:
    def __init__(self, batch, seq, ctx, tile):
        assert seq % tile == 0 and ctx % tile == 0
        self.batch, self.seq, self.ctx, self.tile = batch, seq, ctx, tile
        self.per_batch = seq // tile
        self.ctx_tiles = ctx // tile
        self.n_tiles = batch * self.per_batch

    def mod_index(self, i, row):
        b = i // self.per_batch
        is_ctx = (i % self.per_batch) < self.ctx_tiles
        return jnp.where(is_ctx, self.batch, b) * 6 + row


def _mod_kernel(c_ref, w_ref, b_ref, o_ref):
    a = _silu(c_ref[...]).astype(BF16)
    o_ref[0] = _dot(a, w_ref[0].astype(BF16)) + b_ref[0]


def _modulation(cond, mod_w, mod_b):
    depth, d, n = mod_w.shape
    rows = cond.shape[0]
    tn = 1536
    return pl.pallas_call(
        _mod_kernel,
        out_shape=jax.ShapeDtypeStruct((depth, rows, n), F32),
        grid=(depth, n // tn),
        in_specs=[pl.BlockSpec((rows, d), lambda l, j: (0, 0)),
                  pl.BlockSpec((1, d, tn), lambda l, j: (l, 0, j)),
                  pl.BlockSpec((1, 1, tn), lambda l, j: (l, 0, j))],
        out_specs=pl.BlockSpec((1, rows, tn), lambda l, j: (l, 0, j)),
        compiler_params=_params("parallel", "parallel"),
        name="modulation",
    )(cond, mod_w, mod_b.reshape(depth, 1, n))


def _norm_mod_kernel(h_ref, g_ref, shift_ref, scale_ref, o_ref):
    u = _rms(h_ref[...]) * g_ref[...]
    o_ref[...] = (u * (1.0 + scale_ref[0]) + shift_ref[0]).astype(o_ref.dtype)


def _norm_mod(h, g, mods, rows, shift_row, out_dtype):
    n, d = h.shape
    tm = rows.tile
    return pl.pallas_call(
        _norm_mod_kernel,
        out_shape=jax.ShapeDtypeStruct((n, d), out_dtype),
        grid=(rows.n_tiles,),
        in_specs=[pl.BlockSpec((tm, d), lambda i: (i, 0)),
                  pl.BlockSpec((1, d), lambda i: (0, 0)),
                  pl.BlockSpec((1, 1, d), lambda i: (rows.mod_index(i, shift_row), 0, 0)),
                  pl.BlockSpec((1, 1, d), lambda i: (rows.mod_index(i, shift_row + 1), 0, 0))],
        out_specs=pl.BlockSpec((tm, d), lambda i: (i, 0)),
        compiler_params=_params("parallel"),
        name="norm_mod",
    )(h, g.reshape(1, d), mods, mods)


def _mm_kernel(a_ref, w_ref, o_ref):
    o_ref[...] = _dot(a_ref[...].astype(BF16), w_ref[...]).astype(o_ref.dtype)


def _matmul(a, w, out_dtype, tm=512, tn=None):
    m, k = a.shape
    n = w.shape[1]
    tn = n if tn is None else tn
    assert m % tm == 0 and n % tn == 0
    return pl.pallas_call(
        _mm_kernel,
        out_shape=jax.ShapeDtypeStruct((m, n), out_dtype),
        grid=(m // tm, n // tn),
        in_specs=[pl.BlockSpec((tm, k), lambda i, j: (i, 0)),
                  pl.BlockSpec((k, tn), lambda i, j: (0, j))],
        out_specs=pl.BlockSpec((tm, tn), lambda i, j: (i, j)),
        compiler_params=_params("parallel", "parallel"),
        name="matmul",
    )(a, w)


def _mm_res_kernel(a_ref, w_ref, res_ref, gate_ref, o_ref):
    y = _dot(a_ref[...].astype(BF16), w_ref[...])
    o_ref[...] = res_ref[...] + gate_ref[0] * y


def _matmul_residual(a, w, res, mods, rows, gate_row):
    m, k = a.shape
    n = w.shape[1]
    tm = rows.tile
    return pl.pallas_call(
        _mm_res_kernel,
        out_shape=jax.ShapeDtypeStruct((m, n), F32),
        grid=(rows.n_tiles,),
        in_specs=[pl.BlockSpec((tm, k), lambda i: (i, 0)),
                  pl.BlockSpec((k, n), lambda i: (0, 0)),
                  pl.BlockSpec((tm, n), lambda i: (i, 0)),
                  pl.BlockSpec((1, 1, n), lambda i: (rows.mod_index(i, gate_row), 0, 0))],
        out_specs=pl.BlockSpec((tm, n), lambda i: (i, 0)),
        compiler_params=_params("parallel"),
        name="matmul_residual",
    )(a, w, res, mods)


def _rope_tables(n_lat, ctx):
    rows_ = n_lat // GRID_W
    row, col = jnp.meshgrid(jnp.arange(rows_), jnp.arange(GRID_W), indexing="ij")
    pos = jnp.stack([row.reshape(-1), col.reshape(-1)], axis=-1).astype(F32)
    n_freq = HEAD_DIM // 4
    inv_freq = ROPE_THETA ** (-jnp.arange(n_freq, dtype=F32) / n_freq)
    ang = jnp.broadcast_to(pos[:, :, None, None] * inv_freq, (n_lat, 2, 2, n_freq)).reshape(n_lat, HEAD_DIM)
    cos = jnp.concatenate([jnp.ones((ctx, HEAD_DIM), F32), jnp.cos(ang)], axis=0)
    sin = jnp.concatenate([jnp.zeros((ctx, HEAD_DIM), F32), jnp.sin(ang)], axis=0)
    return cos, sin


def _rope(x, cos, sin_signed):
    lane = lax.broadcasted_iota(jnp.int32, x.shape, 1)
    first = (lane % 64) < 32
    partner = jnp.where(first, pltpu.roll(x, HEAD_DIM - 32, 1), pltpu.roll(x, 32, 1))
    return x * cos + partner * sin_signed


def _qkv_kernel(u_ref, w_ref, cos_ref, sin_ref, qg_ref, kg_ref, q_ref, kt_ref, v_ref):
    acc = _dot(u_ref[...], w_ref[...])
    cos = cos_ref[...]
    lane = lax.broadcasted_iota(jnp.int32, cos.shape, 1)
    sin_signed = jnp.where((lane % 64) < 32, -sin_ref[...], sin_ref[...])
    q_scale = HEAD_DIM ** -0.5
    for h in range(ATTN_HEADS):
        x = _rms(acc[:, h * HEAD_DIM:(h + 1) * HEAD_DIM]) * qg_ref[...]
        q_ref[:, h * HEAD_DIM:(h + 1) * HEAD_DIM] = (_rope(x, cos, sin_signed) * q_scale).astype(BF16)
    k0 = ATTN_HEADS * HEAD_DIM
    v0 = k0 + ATTN_KV_HEADS * HEAD_DIM
    for h in range(ATTN_KV_HEADS):
        x = _rms(acc[:, k0 + h * HEAD_DIM:k0 + (h + 1) * HEAD_DIM]) * kg_ref[...]
        kt_ref[0, h * HEAD_DIM:(h + 1) * HEAD_DIM, :] = _rope(x, cos, sin_signed).T.astype(BF16)
    v_ref[...] = acc[:, v0:].astype(BF16)


def _qkv_project(u, w_qkv, cos, sin, q_gain, k_gain, rows):
    n, d = u.shape
    tm = rows.tile
    nq = ATTN_HEADS * HEAD_DIM
    nkv = ATTN_KV_HEADS * HEAD_DIM
    pb = rows.per_batch
    return pl.pallas_call(
        _qkv_kernel,
        out_shape=(jax.ShapeDtypeStruct((n, nq), BF16),
                   jax.ShapeDtypeStruct((rows.batch, nkv, rows.seq), BF16),
                   jax.ShapeDtypeStruct((n, nkv), BF16)),
        grid=(rows.n_tiles,),
        in_specs=[pl.BlockSpec((tm, d), lambda i: (i, 0)),
                  pl.BlockSpec((d, nq + 2 * nkv), lambda i: (0, 0)),
                  pl.BlockSpec((tm, HEAD_DIM), lambda i: (i % pb, 0)),
                  pl.BlockSpec((tm, HEAD_DIM), lambda i: (i % pb, 0)),
                  pl.BlockSpec((1, HEAD_DIM), lambda i: (0, 0)),
                  pl.BlockSpec((1, HEAD_DIM), lambda i: (0, 0))],
        out_specs=(pl.BlockSpec((tm, nq), lambda i: (i, 0)),
                   pl.BlockSpec((1, nkv, tm), lambda i: (i // pb, 0, i % pb)),
                   pl.BlockSpec((tm, nkv), lambda i: (i, 0))),
        compiler_params=_params("parallel"),
        name="qkv_project",
    )(u, w_qkv, cos, sin, q_gain.reshape(1, HEAD_DIM), k_gain.reshape(1, HEAD_DIM))


def _flash_kernel(q_ref, kt_ref, v_ref, o_ref, m_sc, l_sc, acc_sc, *, tk):
    tq = q_ref.shape[0]
    n_keys = v_ref.shape[0]
    q = jnp.concatenate([q_ref[:, g * HEAD_DIM:(g + 1) * HEAD_DIM] for g in range(ATTN_GROUP)], axis=0)
    m_sc[...] = jnp.full(m_sc.shape, -jnp.inf, F32)
    l_sc[...] = jnp.zeros(l_sc.shape, F32)
    acc_sc[...] = jnp.zeros(acc_sc.shape, F32)

    def body(c, carry):
        off = pl.multiple_of(c * tk, tk)
        s = _dot(q, kt_ref[0, :, pl.ds(off, tk)])
        m_prev = m_sc[...]
        m_new = jnp.maximum(m_prev, jnp.max(s, axis=-1, keepdims=True))
        alpha = jnp.exp(m_prev - m_new)
        p = jnp.exp(s - m_new)
        l_sc[...] = alpha * l_sc[...] + jnp.sum(p, axis=-1, keepdims=True)
        acc_sc[...] = alpha * acc_sc[...] + _dot(p.astype(BF16), v_ref[pl.ds(off, tk), :])
        m_sc[...] = m_new
        return carry

    lax.fori_loop(0, n_keys // tk, body, 0)
    out = acc_sc[...] / l_sc[...]
    for g in range(ATTN_GROUP):
        o_ref[:, g * HEAD_DIM:(g + 1) * HEAD_DIM] = out[g * tq:(g + 1) * tq].astype(o_ref.dtype)


def _flash(q, kt, v, rows, n_keys, q_tiles, o_prev=None):
    n = q.shape[0]
    tq = rows.tile
    pb = rows.per_batch
    width = ATTN_GROUP * HEAD_DIM
    tk = n_keys
    for cand in (768, 512, 384, 256, 128):
        if n_keys % cand == 0:
            tk = cand
            break
    key_blocks = rows.seq // n_keys
    in_specs = [pl.BlockSpec((tq, width), lambda b, h, i: (b * pb + i, h)),
                pl.BlockSpec((1, HEAD_DIM, n_keys), lambda b, h, i: (b, h, 0)),
                pl.BlockSpec((n_keys, HEAD_DIM), lambda b, h, i: (b * key_blocks, h))]
    args = [q, kt, v]
    aliases = {}
    if o_prev is not None:
        in_specs.append(pl.BlockSpec(memory_space=pl.ANY))
        args.append(o_prev)
        aliases = {3: 0}

    def kernel(q_ref, kt_ref, v_ref, *rest):
        o_ref, m_sc, l_sc, acc_sc = rest[-4:]
        _flash_kernel(q_ref, kt_ref, v_ref, o_ref, m_sc, l_sc, acc_sc, tk=tk)

    return pl.pallas_call(
        kernel,
        out_shape=jax.ShapeDtypeStruct((n, ATTN_HEADS * HEAD_DIM), BF16),
        grid=(rows.batch, ATTN_KV_HEADS, q_tiles),
        in_specs=in_specs,
        out_specs=pl.BlockSpec((tq, width), lambda b, h, i: (b * pb + i, h)),
        scratch_shapes=[pltpu.VMEM((ATTN_GROUP * tq, 1), F32),
                        pltpu.VMEM((ATTN_GROUP * tq, 1), F32),
                        pltpu.VMEM((ATTN_GROUP * tq, HEAD_DIM), F32)],
        input_output_aliases=aliases,
        compiler_params=_params("parallel", "parallel", "arbitrary"),
        name="flash_attention",
    )(*args)


def _attention_layer(h, mods, rows, cos, sin, norm_g, w_qkv, q_gain, k_gain, w_o, need_ctx):
    u = _norm_mod(h, norm_g, mods, rows, 0, BF16)
    q, kt, v = _qkv_project(u, w_qkv.astype(BF16), cos, sin, q_gain, k_gain, rows)
    o = _flash(q, kt, v, rows, rows.seq, rows.per_batch)
    if need_ctx:
        o = _flash(q, kt, v, rows, rows.ctx, rows.ctx_tiles, o_prev=o)
    return _matmul_residual(o, w_o.astype(BF16), h, mods, rows, 2)


def _router_kernel(h_ref, g_ref, shift_ref, scale_ref, w_ref, b_ref, t_ref, comb_ref):
    t = _rms(h_ref[...]) * g_ref[...]
    t = t * (1.0 + scale_ref[0]) + shift_ref[0]
    t_ref[...] = t.astype(BF16)
    t_hi, t_lo = _split_bf16(t)
    w_hi, w_lo = _split_bf16(w_ref[...])
    logits = _dot(t_hi, w_hi) + _dot(t_lo, w_hi) + _dot(t_hi, w_lo) + b_ref[...]
    lane = lax.broadcasted_iota(jnp.int32, logits.shape, 1)
    neg = -jnp.inf
    is_group = (lane >= MOE_EXPERTS) & (lane < MOE_EXPERTS + MOE_GROUPS)
    gl = jnp.where(is_group, logits, neg)
    g_max = jnp.max(gl, axis=-1, keepdims=True)
    g_idx = jnp.min(jnp.where(gl == g_max, lane, LANES), axis=-1, keepdims=True) - MOE_EXPERTS
    g_w = 1.0 / jnp.sum(jnp.exp(gl - g_max), axis=-1, keepdims=True)
    in_group = (lane >= g_idx * MOE_PER_GROUP) & (lane < (g_idx + 1) * MOE_PER_GROUP)
    el = jnp.where(in_group, logits, neg)
    m1 = jnp.max(el, axis=-1, keepdims=True)
    i1 = jnp.min(jnp.where(el == m1, lane, LANES), axis=-1, keepdims=True)
    el2 = jnp.where(lane == i1, neg, el)
    m2 = jnp.max(el2, axis=-1, keepdims=True)
    i2 = jnp.min(jnp.where(el2 == m2, lane, LANES), axis=-1, keepdims=True)
    e2 = jnp.exp(m2 - m1)
    w1 = 1.0 / (1.0 + e2)
    w2 = e2 * w1
    comb_ref[...] = jnp.where(lane == i1, g_w * w1, 0.0) + jnp.where(lane == i2, g_w * w2, 0.0)


def _router(h, norm_g, mods, rows, w_rt, b_rt):
    n, d = h.shape
    tm = rows.tile
    return pl.pallas_call(
        _router_kernel,
        out_shape=(jax.ShapeDtypeStruct((n, d), BF16), jax.ShapeDtypeStruct((n, LANES), F32)),
        grid=(rows.n_tiles,),
        in_specs=[pl.BlockSpec((tm, d), lambda i: (i, 0)),
                  pl.BlockSpec((1, d), lambda i: (0, 0)),
                  pl.BlockSpec((1, 1, d), lambda i: (rows.mod_index(i, 3), 0, 0)),
                  pl.BlockSpec((1, 1, d), lambda i: (rows.mod_index(i, 4), 0, 0)),
                  pl.BlockSpec((d, LANES), lambda i: (0, 0)),
                  pl.BlockSpec((1, LANES), lambda i: (0, 0))],
        out_specs=(pl.BlockSpec((tm, d), lambda i: (i, 0)),
                   pl.BlockSpec((tm, LANES), lambda i: (i, 0))),
        compiler_params=_params("parallel"),
        name="moe_router",
    )(h, norm_g.reshape(1, d), mods, mods, w_rt, b_rt)


def _experts_kernel(t_ref, comb_ref, wg_ref, wu_ref, wd_ref, res_ref, gate_ref, o_ref, acc_sc):
    e = pl.program_id(1)
    t = t_ref[...]
    hid = _silu(_dot(t, wg_ref[0])) * _dot(t, wu_ref[0])
    comb = comb_ref[...]
    lane = lax.broadcasted_iota(jnp.int32, comb.shape, 1)
    ce = jnp.sum(jnp.where(lane == e, comb, 0.0), axis=-1, keepdims=True)
    y = _dot((hid * ce).astype(BF16), wd_ref[0])

    @pl.when(e == 0)
    def _():
        acc_sc[...] = y

    @pl.when(e > 0)
    def _():
        acc_sc[...] += y

    @pl.when(e == pl.num_programs(1) - 1)
    def _():
        o_ref[...] = res_ref[...] + gate_ref[0] * acc_sc[...]


def _experts_dense(t, comb, w_gate, w_up, w_down, res, mods, rows, tm):
    n, d = t.shape
    ne, _, hid = w_gate.shape
    big = _Rows(rows.batch, rows.seq, rows.ctx, tm)
    return pl.pallas_call(
        _experts_kernel,
        out_shape=jax.ShapeDtypeStruct((n, d), F32),
        grid=(big.n_tiles, ne),
        in_specs=[pl.BlockSpec((tm, d), lambda i, e: (i, 0)),
                  pl.BlockSpec((tm, LANES), lambda i, e: (i, 0)),
                  pl.BlockSpec((1, d, hid), lambda i, e: (e, 0, 0)),
                  pl.BlockSpec((1, d, hid), lambda i, e: (e, 0, 0)),
                  pl.BlockSpec((1, hid, d), lambda i, e: (e, 0, 0)),
                  pl.BlockSpec((tm, d), lambda i, e: (i, 0)),
                  pl.BlockSpec((1, 1, d), lambda i, e: (big.mod_index(i, 5), 0, 0))],
        out_specs=pl.BlockSpec((tm, d), lambda i, e: (i, 0)),
        scratch_shapes=[pltpu.VMEM((tm, d), F32)],
        compiler_params=_params("parallel", "arbitrary"),
        name="moe_experts",
    )(t, comb, w_gate, w_up, w_down, res, mods)


def _moe_layer(h, mods, rows, norm_g, w_group, b_group, w_router, b_router, w_gate, w_up, w_down):
    d = h.shape[1]
    pad = LANES - MOE_EXPERTS - MOE_GROUPS
    w_rt = jnp.concatenate([w_router, w_group, jnp.zeros((d, pad), F32)], axis=1)
    b_rt = jnp.concatenate([b_router, b_group, jnp.zeros((pad,), F32)]).reshape(1, LANES)
    t, comb = _router(h, norm_g, mods, rows, w_rt, b_rt)
    return _experts_dense(t, comb, w_gate.astype(BF16), w_up.astype(BF16), w_down.astype(BF16),
                          h, mods, rows, ROW_TILE)


def _scan_block(step, n_ctx_blocks, n_blocks, rev):
    if not rev:
        return step
    return jnp.where(step < n_ctx_blocks, n_ctx_blocks - 1 - step, 2 * n_ctx_blocks + n_blocks - 1 - step)


def _s5_disc_kernel(a_re_ref, a_im_ref, ldt_ref, b_re_ref, b_im_ref, abr_ref, abi_ref, bbr_ref, bbi_ref):
    a_re, a_im = a_re_ref[...], a_im_ref[...]
    dt = jnp.exp(ldt_ref[...])
    mag = jnp.exp(a_re * dt)
    ab_re = mag * jnp.cos(a_im * dt)
    ab_im = mag * jnp.sin(a_im * dt)
    den = a_re * a_re + a_im * a_im
    num_re = ab_re - 1.0
    coef_re = (num_re * a_re + ab_im * a_im) / den
    coef_im = (ab_im * a_re - num_re * a_im) / den
    abr_ref[...] = ab_re
    abi_ref[...] = ab_im
    for c in range(S5_GROUP):
        b_re, b_im = b_re_ref[c], b_im_ref[c]
        bbr_ref[c] = coef_re * b_re - coef_im * b_im
        bbi_ref[c] = coef_re * b_im + coef_im * b_re


def _s5_discretise(a_re, a_im, log_dt, b_re, b_im):
    nd, g, p = a_re.shape
    n = nd * g
    bt = lambda b: jnp.transpose(b, (3, 0, 1, 2)).reshape(S5_GROUP, n, p)
    shp = jax.ShapeDtypeStruct((n, p), F32)
    shp_b = jax.ShapeDtypeStruct((S5_GROUP, n, p), F32)
    return pl.pallas_call(_s5_disc_kernel, out_shape=(shp, shp, shp_b, shp_b), name="s5_discretise")(
        a_re.reshape(n, p), a_im.reshape(n, p), log_dt.reshape(n, 1), bt(b_re), bt(b_im))


def _s5_scan_kernel(u_ref, bm_ref, cm_ref, ar_ref, ai_ref, y_ref, bu_sc, st_sc, *, nb, steps, rev):
    @pl.when(pl.program_id(0) == 0)
    def _():
        st_sc[...] = jnp.zeros(st_sc.shape, F32)

    n_slabs = u_ref.shape[1] // LANES
    ns = S5_SLAB_STATE
    for s in range(n_slabs):
        us = u_ref[:, s * LANES:(s + 1) * LANES].astype(BF16)
        bu_sc[...] = _dot(us, bm_ref[s])
        ar = jnp.broadcast_to(ar_ref[s], (nb, ns))
        ai = jnp.broadcast_to(ai_ref[s], (nb, ns))

        def step(k, carry):
            xr, xi = carry
            t = (steps - 1 - k) if rev else k
            r0 = pl.multiple_of(t * nb, nb)
            br = bu_sc[pl.ds(r0, nb), 0:ns]
            bi = bu_sc[pl.ds(r0, nb), ns:2 * ns]
            nr = ar * xr - ai * xi + br
            ni = ar * xi + ai * xr + bi
            bu_sc[pl.ds(r0, nb), 0:ns] = nr
            bu_sc[pl.ds(r0, nb), ns:2 * ns] = ni
            return nr, ni

        xr, xi = lax.fori_loop(0, steps, step, (st_sc[s, 0], st_sc[s, 1]), unroll=8)
        st_sc[s, 0] = xr
        st_sc[s, 1] = xi
        y_ref[:, s * LANES:(s + 1) * LANES] = _dot(bu_sc[...].astype(BF16), cm_ref[s])


def _s5_scan(u_tm, bmat, cmat, ab_re, ab_im, nb, n_ctx, rev):
    n, d = u_tm.shape
    steps = S5_STEPS
    blk = steps * nb
    n_blocks = n // blk
    n_ctx_blocks = n_ctx // steps
    n_slabs = d // LANES
    ns = S5_SLAB_STATE
    idx = lambda s: (_scan_block(s, n_ctx_blocks, n_blocks - n_ctx_blocks, rev), 0)
    return pl.pallas_call(
        functools.partial(_s5_scan_kernel, nb=nb, steps=steps, rev=rev),
        out_shape=jax.ShapeDtypeStruct((n, d), F32),
        grid=(n_blocks,),
        in_specs=[pl.BlockSpec((blk, d), idx),
                  pl.BlockSpec((n_slabs, LANES, 2 * ns), lambda s: (0, 0, 0)),
                  pl.BlockSpec((n_slabs, 2 * ns, LANES), lambda s: (0, 0, 0)),
                  pl.BlockSpec((n_slabs, 1, ns), lambda s: (0, 0, 0)),
                  pl.BlockSpec((n_slabs, 1, ns), lambda s: (0, 0, 0))],
        out_specs=pl.BlockSpec((blk, d), idx),
        scratch_shapes=[pltpu.VMEM((blk, 2 * ns), F32),
                        pltpu.VMEM((n_slabs, 2, nb, ns), F32)],
        compiler_params=_params("arbitrary"),
        name="s5_scan_bwd" if rev else "s5_scan_fwd",
    )(u_tm, bmat, cmat, ab_re, ab_im)


def _gelu_tanh(x):
    return 0.5 * x * (1.0 + jnp.tanh(math.sqrt(2.0 / math.pi) * (x + 0.044715 * (x * x * x))))


def _s5_glu_kernel(yf_ref, yb_ref, u_ref, d_ref, wa_ref, wb_ref, ba_ref, bb_ref, o_ref):
    y = yf_ref[...] + yb_ref[...] + u_ref[...] * d_ref[...]
    z = _gelu_tanh(y).astype(BF16)
    a = _dot(z, wa_ref[...]) + ba_ref[...]
    b = _dot(z, wb_ref[...]) + bb_ref[...]
    o_ref[...] = a * _sigmoid(b)


def _s5_glu(y_f, y_b, u, d_skip, w_glu, b_glu, tm):
    n, d = u.shape
    row = pl.BlockSpec((tm, d), lambda i: (i, 0))
    vec = pl.BlockSpec((1, d), lambda i: (0, 0))
    return pl.pallas_call(
        _s5_glu_kernel,
        out_shape=jax.ShapeDtypeStruct((n, d), F32),
        grid=(n // tm,),
        in_specs=[row, row, row, vec,
                  pl.BlockSpec((d, d), lambda i: (0, 0)),
                  pl.BlockSpec((d, d), lambda i: (0, 1)),
                  vec,
                  pl.BlockSpec((1, d), lambda i: (0, 1))],
        out_specs=row,
        compiler_params=_params("parallel"),
        name="s5_glu",
    )(y_f, y_b, u, d_skip.reshape(1, d), w_glu, w_glu, b_glu.reshape(1, 2 * d), b_glu.reshape(1, 2 * d))


def _residual_kernel(h_ref, y_ref, gate_ref, o_ref):
    o_ref[...] = h_ref[...] + gate_ref[0] * y_ref[...]


def _residual(h, y, mods, rows, gate_row):
    n, d = h.shape
    tm = rows.tile
    row = pl.BlockSpec((tm, d), lambda i: (i, 0))
    return pl.pallas_call(
        _residual_kernel,
        out_shape=jax.ShapeDtypeStruct((n, d), F32),
        grid=(rows.n_tiles,),
        in_specs=[row, row, pl.BlockSpec((1, 1, d), lambda i: (rows.mod_index(i, gate_row), 0, 0))],
        out_specs=row,
        compiler_params=_params("parallel"),
        name="residual",
    )(h, y, mods)


def _s5_layer(h, mods, rows, norm_g, a_re, a_im, log_dt, b_re, b_im, c_re, c_im, d_skip, w_glu, b_glu):
    nb, seq = rows.batch, rows.seq
    d = h.shape[1]
    n_slabs = d // LANES
    sg = S5_SLAB_GROUPS
    ab_re, ab_im, bb_re, bb_im = _s5_discretise(a_re, a_im, log_dt, b_re, b_im)
    eye = jnp.eye(sg, dtype=F32)

    def b_blocks(bb):
        t = bb.reshape(S5_GROUP, 2, n_slabs, sg, S5_STATE)
        t = jnp.einsum("cdsgp,gh->dsgchp", t, eye)
        return t.reshape(2, n_slabs, LANES, S5_SLAB_STATE)

    def c_blocks(cc):
        t = cc.reshape(2, n_slabs, sg, S5_GROUP, S5_STATE)
        t = jnp.einsum("dsgcp,gh->dshpgc", t, eye)
        return t.reshape(2, n_slabs, S5_SLAB_STATE, LANES)

    bmat = jnp.concatenate([b_blocks(bb_re), b_blocks(bb_im)], axis=-1).astype(BF16)
    cmat = jnp.concatenate([c_blocks(c_re), -c_blocks(c_im)], axis=2).astype(BF16)
    ab_re = ab_re.reshape(2, n_slabs, 1, S5_SLAB_STATE)
    ab_im = ab_im.reshape(2, n_slabs, 1, S5_SLAB_STATE)

    u = _norm_mod(h, norm_g, mods, rows, 0, F32)
    u_tm = jnp.transpose(u.reshape(nb, seq, d), (1, 0, 2)).reshape(seq * nb, d)
    y_f = _s5_scan(u_tm, bmat[0], cmat[0], ab_re[0], ab_im[0], nb, rows.ctx, False)
    y_b = _s5_scan(u_tm, bmat[1], cmat[1], ab_re[1], ab_im[1], nb, rows.ctx, True)
    mix_tm = _s5_glu(y_f, y_b, u_tm, d_skip, w_glu.astype(BF16), b_glu, S5_STEPS * nb)
    mix = jnp.transpose(mix_tm.reshape(seq, nb, d), (1, 0, 2)).reshape(nb * seq, d)
    return _residual(h, mix, mods, rows, 2)


def _ssd_conv_kernel(x_ref, w_ref, b_ref, o_ref, pad_sc, *, ctx, n_lat, chunk):
    tc = x_ref.shape[2]
    zeros = jnp.zeros((8, tc), F32)
    lat0 = 16 + ctx
    pad_sc[0:8, :] = zeros
    pad_sc[8 + ctx:lat0, :] = zeros
    pad_sc[lat0 + n_lat:lat0 + n_lat + 8, :] = zeros
    half = SSD_CONV // 2
    segments = ((0, ctx, 8), (ctx, n_lat, lat0))
    for src0, length, dst0 in segments:
        for r in range(0, length, chunk):
            pad_sc[dst0 + r:dst0 + r + chunk, :] = x_ref[0, src0 + r:src0 + r + chunk, :]
    w = w_ref[...]
    for src0, length, dst0 in segments:
        for r in range(0, length, chunk):
            acc = jnp.broadcast_to(b_ref[...], (chunk, tc))
            for k in range(SSD_CONV):
                lo = dst0 + r + k - half
                acc = acc + w[k:k + 1, :] * pad_sc[lo:lo + chunk, :]
            o_ref[0, src0 + r:src0 + r + chunk, :] = _silu(acc)


def _ssd_conv(xbc, conv_w, conv_b, ctx):
    nb, seq, ch = xbc.shape
    tc = LANES
    n_lat = seq - ctx
    return pl.pallas_call(
        functools.partial(_ssd_conv_kernel, ctx=ctx, n_lat=n_lat, chunk=ROW_TILE),
        out_shape=jax.ShapeDtypeStruct((nb, seq, ch), F32),
        grid=(nb, ch // tc),
        in_specs=[pl.BlockSpec((1, seq, tc), lambda b, j: (b, 0, j)),
                  pl.BlockSpec((SSD_CONV, tc), lambda b, j: (0, j)),
                  pl.BlockSpec((1, tc), lambda b, j: (0, j))],
        out_specs=pl.BlockSpec((1, seq, tc), lambda b, j: (b, 0, j)),
        scratch_shapes=[pltpu.VMEM((seq + 24, tc), F32)],
        compiler_params=_params("parallel", "parallel"),
        name="ssd_conv",
    )(xbc, conv_w, conv_b.reshape(1, ch))


def _softplus(x):
    return jnp.maximum(x, 0.0) + jnp.log(1.0 + jnp.exp(-jnp.abs(x)))


def _ssd_scan_kernel(xbc_ref, dt_ref, dtb_ref, a_ref, y_ref, st_sc, *, rev, col0, n_heads):
    @pl.when(pl.program_id(1) == 0)
    def _():
        st_sc[...] = jnp.zeros(st_sc.shape, F32)

    q = xbc_ref.shape[1]
    hpg = n_heads // SSD_GROUPS
    inner = n_heads * SSD_HEAD_DIM
    dt = _softplus(dt_ref[0] + dtb_ref[...])
    a = dt * a_ref[...]
    row = lax.broadcasted_iota(jnp.int32, (q, q), 0)
    col = lax.broadcasted_iota(jnp.int32, (q, q), 1)
    tri = (col >= row) if rev else (col <= row)
    tri_b = jnp.where(tri, 1.0, 0.0).astype(BF16)
    a_hi, a_lo = _split_bf16(a)
    cum = _dot(tri_b, a_hi) + _dot(tri_b, a_lo)
    cum_t = cum.T
    total = cum[0:1, :] if rev else cum[q - 1:q, :]
    for g in range(SSD_GROUPS):
        bg = xbc_ref[0, :, inner + g * SSD_STATE:inner + (g + 1) * SSD_STATE]
        cg = xbc_ref[0, :, inner + (SSD_GROUPS + g) * SSD_STATE:inner + (SSD_GROUPS + g + 1) * SSD_STATE]
        cg_b = cg.astype(BF16)
        cb = lax.dot_general(cg_b, bg.astype(BF16), (((1,), (1,)), ((), ())), preferred_element_type=F32)
        bg_t = bg.T
        for hh in range(hpg):
            hd = g * hpg + hh
            c = col0 + hd
            cum_c = cum[:, c:c + 1]
            cum_r = cum_t[c:c + 1, :]
            decay = jnp.exp(jnp.where(tri, cum_c - cum_r, -jnp.inf))
            xdt = (xbc_ref[0, :, hd * SSD_HEAD_DIM:(hd + 1) * SSD_HEAD_DIM] * dt[:, c:c + 1]).astype(BF16)
            state = st_sc[hd]
            y = _dot((cb * decay).astype(BF16), xdt) + jnp.exp(cum_c) * _dot(cg_b, state.astype(BF16))
            y_ref[0, :, hd * SSD_HEAD_DIM:(hd + 1) * SSD_HEAD_DIM] = y
            tot = total[:, c:c + 1]
            tail = jnp.exp(tot - cum_r)
            st_sc[hd] = jnp.exp(tot) * state + _dot((bg_t * tail).astype(BF16), xdt)


def _ssd_scan(xbc, dt_raw, dt_bias, a_neg, ctx, rev, n_heads):
    nb, seq, ch = xbc.shape
    q = SSD_CHUNK
    n_blocks = seq // q
    n_ctx_blocks = ctx // q
    inner = n_heads * SSD_HEAD_DIM
    blk = lambda b, s: (b, _scan_block(s, n_ctx_blocks, n_blocks - n_ctx_blocks, rev), 0)
    return pl.pallas_call(
        functools.partial(_ssd_scan_kernel, rev=rev, col0=n_heads if rev else 0, n_heads=n_heads),
        out_shape=jax.ShapeDtypeStruct((nb, seq, inner), F32),
        grid=(nb, n_blocks),
        in_specs=[pl.BlockSpec((1, q, ch), blk),
                  pl.BlockSpec((1, q, LANES), blk),
                  pl.BlockSpec((1, LANES), lambda b, s: (0, 0)),
                  pl.BlockSpec((1, LANES), lambda b, s: (0, 0))],
        out_specs=pl.BlockSpec((1, q, inner), blk),
        scratch_shapes=[pltpu.VMEM((n_heads, SSD_STATE, SSD_HEAD_DIM), F32)],
        compiler_params=_params("parallel", "arbitrary"),
        name="ssd_scan_bwd" if rev else "ssd_scan_fwd",
    )(xbc, dt_raw, dt_bias, a_neg)


def _ssd_finish_kernel(yf_ref, yb_ref, x_ref, z_ref, d_ref, g_ref, o_ref):
    y = yf_ref[...] + yb_ref[...] + x_ref[...] * d_ref[...]
    o_ref[...] = (_rms(y * _silu(z_ref[...])) * g_ref[...]).astype(o_ref.dtype)


def _ssd_finish(y_f, y_b, xbc, z, d_skip, norm_g, tm):
    n, inner = z.shape
    row = pl.BlockSpec((tm, inner), lambda i: (i, 0))
    vec = pl.BlockSpec((1, inner), lambda i: (0, 0))
    return pl.pallas_call(
        _ssd_finish_kernel,
        out_shape=jax.ShapeDtypeStruct((n, inner), BF16),
        grid=(n // tm,),
        in_specs=[row, row, row, row, vec, vec],
        out_specs=row,
        compiler_params=_params("parallel"),
        name="ssd_finish",
    )(y_f, y_b, xbc, z, d_skip, norm_g)


def _ssd_layer(h, mods, rows, norm_g, w_in, conv_w, conv_b, dt_bias, a_log, d_skip, ssd_norm_g, w_out):
    nb, seq = rows.batch, rows.seq
    n = nb * seq
    n_heads = dt_bias.shape[1]
    inner = n_heads * SSD_HEAD_DIM
    conv_ch = inner + 2 * SSD_GROUPS * SSD_STATE
    u = _norm_mod(h, norm_g, mods, rows, 0, BF16)
    w_in = w_in.astype(BF16)
    d = w_in.shape[0]
    z = _matmul(u, w_in[:, :inner], F32, tn=1024)
    xbc = _matmul(u, w_in[:, inner:inner + conv_ch], F32, tn=1024)
    w_dt = jnp.concatenate([w_in[:, inner + conv_ch:], jnp.zeros((d, LANES - 2 * n_heads), BF16)], axis=1)
    dt_raw = _matmul(u, w_dt, F32)
    xbc = _ssd_conv(xbc.reshape(nb, seq, conv_ch), conv_w, conv_b, rows.ctx)
    pad = jnp.zeros((LANES - 2 * n_heads,), F32)
    dtb = jnp.concatenate([dt_bias.reshape(-1), pad]).reshape(1, LANES)
    a_neg = jnp.concatenate([a_log.reshape(-1), pad]).reshape(1, LANES)
    a_neg = _neg_exp(a_neg, 2 * n_heads)
    dt3 = dt_raw.reshape(nb, seq, LANES)
    y_f = _ssd_scan(xbc, dt3, dtb, a_neg, rows.ctx, False, n_heads)
    y_b = _ssd_scan(xbc, dt3, dtb, a_neg, rows.ctx, True, n_heads)
    dsk = jnp.repeat(d_skip, SSD_HEAD_DIM).reshape(1, inner)
    g = _ssd_finish(y_f.reshape(n, inner), y_b.reshape(n, inner), xbc.reshape(n, conv_ch), z, dsk,
                    ssd_norm_g.reshape(1, inner), rows.tile)
    return _matmul_residual(g, w_out.astype(BF16), h, mods, rows, 2)


def _neg_exp_kernel(x_ref, o_ref, *, valid):
    lane = lax.broadcasted_iota(jnp.int32, x_ref.shape, 1)
    o_ref[...] = jnp.where(lane < valid, -jnp.exp(x_ref[...]), 0.0)


def _neg_exp(x, valid):
    return pl.pallas_call(functools.partial(_neg_exp_kernel, valid=valid),
                          out_shape=jax.ShapeDtypeStruct(x.shape, F32), name="ssd_decay_rate")(x)


def kernel(x, c, ctx, c_ctx, mod_w, mod_b, norm1_g, norm2_g, attn_w_qkv, attn_q_gain, attn_k_gain, attn_w_o, s5_a_re, s5_a_im, s5_log_dt, s5_b_re, s5_b_im, s5_c_re, s5_c_im, s5_d, s5_w_glu, s5_b_glu, ssd_w_in, ssd_conv_w, ssd_conv_b, ssd_dt_bias, ssd_a_log, ssd_d, ssd_norm_g, ssd_w_out, moe_w_group, moe_b_group, moe_w_router, moe_b_router, moe_w_gate, moe_w_up, moe_w_down):
    batch, n_lat, d = x.shape
    n_ctx = ctx.shape[1]
    depth = mod_w.shape[0]
    seq = n_ctx + n_lat
    rows = _Rows(batch, seq, n_ctx, ROW_TILE)

    n_cond = -(-(batch + 1) // 8) * 8
    cond = jnp.concatenate([c, c_ctx[None, :], jnp.zeros((n_cond - batch - 1, d), F32)], axis=0)
    mods_all = _modulation(cond, mod_w, mod_b).reshape(depth, n_cond * 6, 1, d)

    cos, sin = _rope_tables(n_lat, n_ctx)
    h = jnp.concatenate([ctx, x], axis=1).reshape(batch * seq, d)

    for i in range(depth):
        need_ctx = i < depth - 1
        kind, j = i % N_MIXERS, i // N_MIXERS
        mods = mods_all[i]
        if kind == 0:
            h = _attention_layer(h, mods, rows, cos, sin, norm1_g[i], attn_w_qkv[j], attn_q_gain[j],
                                 attn_k_gain[j], attn_w_o[j], need_ctx)
        elif kind == 1:
            h = _s5_layer(h, mods, rows, norm1_g[i], s5_a_re[j], s5_a_im[j], s5_log_dt[j], s5_b_re[j], s5_b_im[j],
                          s5_c_re[j], s5_c_im[j], s5_d[j], s5_w_glu[j], s5_b_glu[j])
        else:
            h = _ssd_layer(h, mods, rows, norm1_g[i], ssd_w_in[j], ssd_conv_w[j], ssd_conv_b[j], ssd_dt_bias[j],
                           ssd_a_log[j], ssd_d[j], ssd_norm_g[j], ssd_w_out[j])
        h = _moe_layer(h, mods, rows, norm2_g[i], moe_w_group[i], moe_b_group[i], moe_w_router[i],
                       moe_b_router[i], moe_w_gate[i], moe_w_up[i], moe_w_down[i])
    return h.reshape(batch, seq, d)[:, n_ctx:, :]
```

```python
import functools
import math

import jax
import jax.numpy as jnp
from jax import lax
from jax.experimental import pallas as pl
from jax.experimental.pallas import tpu as pltpu

F32 = jnp.float32
BF16 = jnp.bfloat16

NORM_EPS = 1e-6
GRID_W = 64
N_MIXERS = 3

ATTN_HEADS = 8
ATTN_KV_HEADS = 2
ATTN_GROUP = ATTN_HEADS // ATTN_KV_HEADS
HEAD_DIM = 128
ROPE_THETA = 10000.0

S5_GROUP = 16
S5_STATE = 64
S5_SLAB_GROUPS = 8
S5_SLAB_STATE = S5_SLAB_GROUPS * S5_STATE
S5_STEPS = 64

SSD_HEAD_DIM = 64
SSD_GROUPS = 4
SSD_STATE = 128
SSD_CONV = 5
SSD_CHUNK = 128

MOE_GROUPS = 4
MOE_PER_GROUP = 8
MOE_EXPERTS = MOE_GROUPS * MOE_PER_GROUP
MOE_HIDDEN = 256
MOE_TILE = 512

LANES = 128
ROW_TILE = 256
VMEM_LIMIT = 56 * 1024 * 1024


def _params(*sem):
    return pltpu.CompilerParams(dimension_semantics=sem, vmem_limit_bytes=VMEM_LIMIT)


def _rms(x):
    return x * lax.rsqrt(jnp.mean(x * x, axis=-1, keepdims=True) + NORM_EPS)


def _silu(x):
    return x * (1.0 / (1.0 + jnp.exp(-x)))


def _sigmoid(x):
    return 1.0 / (1.0 + jnp.exp(-x))


def _dot(a, b):
    return jnp.dot(a, b, preferred_element_type=F32)


def _split_bf16(x):
    hi = x.astype(BF16)
    lo = (x - hi.astype(F32)).astype(BF16)
    return hi, lo


class _Rows:
    def __init__(self, batch, seq, ctx, tile):
        assert seq % tile == 0 and ctx % tile == 0
        self.batch, self.seq, self.ctx, self.tile = batch, seq, ctx, tile
        self.per_batch = seq // tile
        self.ctx_tiles = ctx // tile
        self.n_tiles = batch * self.per_batch

    def mod_index(self, i, row):
        b = i // self.per_batch
        is_ctx = (i % self.per_batch) < self.ctx_tiles
        return jnp.where(is_ctx, self.batch, b) * 6 + row


def _mod_kernel(c_ref, w_ref, b_ref, o_ref):
    a = _silu(c_ref[...]).astype(BF16)
    o_ref[0] = _dot(a, w_ref[0].astype(BF16)) + b_ref[0]


def _modulation(cond, mod_w, mod_b):
    depth, d, n = mod_w.shape
    rows = cond.shape[0]
    tn = 1536
    return pl.pallas_call(
        _mod_kernel,
        out_shape=jax.ShapeDtypeStruct((depth, rows, n), F32),
        grid=(depth, n // tn),
        in_specs=[pl.BlockSpec((rows, d), lambda l, j: (0, 0)),
                  pl.BlockSpec((1, d, tn), lambda l, j: (l, 0, j)),
                  pl.BlockSpec((1, 1, tn), lambda l, j: (l, 0, j))],
        out_specs=pl.BlockSpec((1, rows, tn), lambda l, j: (l, 0, j)),
        compiler_params=_params("parallel", "parallel"),
        name="modulation",
    )(cond, mod_w, mod_b.reshape(depth, 1, n))


def _norm_mod_kernel(h_ref, g_ref, shift_ref, scale_ref, o_ref):
    u = _rms(h_ref[...]) * g_ref[...]
    o_ref[...] = (u * (1.0 + scale_ref[0]) + shift_ref[0]).astype(o_ref.dtype)


def _norm_mod(h, g, mods, rows, shift_row, out_dtype):
    n, d = h.shape
    tm = rows.tile
    return pl.pallas_call(
        _norm_mod_kernel,
        out_shape=jax.ShapeDtypeStruct((n, d), out_dtype),
        grid=(rows.n_tiles,),
        in_specs=[pl.BlockSpec((tm, d), lambda i: (i, 0)),
                  pl.BlockSpec((1, d), lambda i: (0, 0)),
                  pl.BlockSpec((1, 1, d), lambda i: (rows.mod_index(i, shift_row), 0, 0)),
                  pl.BlockSpec((1, 1, d), lambda i: (rows.mod_index(i, shift_row + 1), 0, 0))],
        out_specs=pl.BlockSpec((tm, d), lambda i: (i, 0)),
        compiler_params=_params("parallel"),
        name="norm_mod",
    )(h, g.reshape(1, d), mods, mods)


def _mm_kernel(a_ref, w_ref, o_ref):
    o_ref[...] = _dot(a_ref[...].astype(BF16), w_ref[...]).astype(o_ref.dtype)


def _matmul(a, w, out_dtype, tm=512, tn=None):
    m, k = a.shape
    n = w.shape[1]
    tn = n if tn is None else tn
    assert m % tm == 0 and n % tn == 0
    return pl.pallas_call(
        _mm_kernel,
        out_shape=jax.ShapeDtypeStruct((m, n), out_dtype),
        grid=(m // tm, n // tn),
        in_specs=[pl.BlockSpec((tm, k), lambda i, j: (i, 0)),
                  pl.BlockSpec((k, tn), lambda i, j: (0, j))],
        out_specs=pl.BlockSpec((tm, tn), lambda i, j: (i, j)),
        compiler_params=_params("parallel", "parallel"),
        name="matmul",
    )(a, w)


def _mm_res_kernel(a_ref, w_ref, res_ref, gate_ref, o_ref):
    y = _dot(a_ref[...].astype(BF16), w_ref[...])
    o_ref[...] = res_ref[...] + gate_ref[0] * y


def _matmul_residual(a, w, res, mods, rows, gate_row):
    m, k = a.shape
    n = w.shape[1]
    tm = rows.tile
    return pl.pallas_call(
        _mm_res_kernel,
        out_shape=jax.ShapeDtypeStruct((m, n), F32),
        grid=(rows.n_tiles,),
        in_specs=[pl.BlockSpec((tm, k), lambda i: (i, 0)),
                  pl.BlockSpec((k, n), lambda i: (0, 0)),
                  pl.BlockSpec((tm, n), lambda i: (i, 0)),
                  pl.BlockSpec((1, 1, n), lambda i: (rows.mod_index(i, gate_row), 0, 0))],
        out_specs=pl.BlockSpec((tm, n), lambda i: (i, 0)),
        compiler_params=_params("parallel"),
        name="matmul_residual",
    )(a, w, res, mods)


def _rope_tables(n_lat, ctx):
    rows_ = n_lat // GRID_W
    row, col = jnp.meshgrid(jnp.arange(rows_), jnp.arange(GRID_W), indexing="ij")
    pos = jnp.stack([row.reshape(-1), col.reshape(-1)], axis=-1).astype(F32)
    n_freq = HEAD_DIM // 4
    inv_freq = ROPE_THETA ** (-jnp.arange(n_freq, dtype=F32) / n_freq)
    ang = jnp.broadcast_to(pos[:, :, None, None] * inv_freq, (n_lat, 2, 2, n_freq)).reshape(n_lat, HEAD_DIM)
    cos = jnp.concatenate([jnp.ones((ctx, HEAD_DIM), F32), jnp.cos(ang)], axis=0)
    sin = jnp.concatenate([jnp.zeros((ctx, HEAD_DIM), F32), jnp.sin(ang)], axis=0)
    return cos, sin


def _rope(x, cos, sin_signed):
    lane = lax.broadcasted_iota(jnp.int32, x.shape, 1)
    first = (lane % 64) < 32
    partner = jnp.where(first, pltpu.roll(x, HEAD_DIM - 32, 1), pltpu.roll(x, 32, 1))
    return x * cos + partner * sin_signed


def _qkv_kernel(u_ref, w_ref, cos_ref, sin_ref, qg_ref, kg_ref, q_ref, kt_ref, v_ref):
    acc = _dot(u_ref[...], w_ref[...])
    cos = cos_ref[...]
    lane = lax.broadcasted_iota(jnp.int32, cos.shape, 1)
    sin_signed = jnp.where((lane % 64) < 32, -sin_ref[...], sin_ref[...])
    q_scale = HEAD_DIM ** -0.5 * math.log2(math.e)
    for h in range(ATTN_HEADS):
        x = _rms(acc[:, h * HEAD_DIM:(h + 1) * HEAD_DIM]) * qg_ref[...]
        q_ref[:, h * HEAD_DIM:(h + 1) * HEAD_DIM] = (_rope(x, cos, sin_signed) * q_scale).astype(BF16)
    k0 = ATTN_HEADS * HEAD_DIM
    v0 = k0 + ATTN_KV_HEADS * HEAD_DIM
    for h in range(ATTN_KV_HEADS):
        x = _rms(acc[:, k0 + h * HEAD_DIM:k0 + (h + 1) * HEAD_DIM]) * kg_ref[...]
        kt_ref[0, h * HEAD_DIM:(h + 1) * HEAD_DIM, :] = _rope(x, cos, sin_signed).T.astype(BF16)
    v_ref[...] = acc[:, v0:].astype(BF16)


def _qkv_project(u, w_qkv, cos, sin, q_gain, k_gain, rows):
    n, d = u.shape
    tm = rows.tile
    nq = ATTN_HEADS * HEAD_DIM
    nkv = ATTN_KV_HEADS * HEAD_DIM
    pb = rows.per_batch
    return pl.pallas_call(
        _qkv_kernel,
        out_shape=(jax.ShapeDtypeStruct((n, nq), BF16),
                   jax.ShapeDtypeStruct((rows.batch, nkv, rows.seq), BF16),
                   jax.ShapeDtypeStruct((n, nkv), BF16)),
        grid=(rows.n_tiles,),
        in_specs=[pl.BlockSpec((tm, d), lambda i: (i, 0)),
                  pl.BlockSpec((d, nq + 2 * nkv), lambda i: (0, 0)),
                  pl.BlockSpec((tm, HEAD_DIM), lambda i: (i % pb, 0)),
                  pl.BlockSpec((tm, HEAD_DIM), lambda i: (i % pb, 0)),
                  pl.BlockSpec((1, HEAD_DIM), lambda i: (0, 0)),
                  pl.BlockSpec((1, HEAD_DIM), lambda i: (0, 0))],
        out_specs=(pl.BlockSpec((tm, nq), lambda i: (i, 0)),
                   pl.BlockSpec((1, nkv, tm), lambda i: (i // pb, 0, i % pb)),
                   pl.BlockSpec((tm, nkv), lambda i: (i, 0))),
        compiler_params=_params("parallel"),
        name="qkv_project",
    )(u, w_qkv, cos, sin, q_gain.reshape(1, HEAD_DIM), k_gain.reshape(1, HEAD_DIM))


def _flash_kernel(q_ref, kt_ref, v_ref, o_ref, m_sc, l_sc, acc_sc, *, tk):
    n_keys = v_ref.shape[0]
    n_tiles = tk // LANES
    m_sc[...] = jnp.full(m_sc.shape, -jnp.inf, F32)
    l_sc[...] = jnp.zeros(l_sc.shape, F32)
    acc_sc[...] = jnp.zeros(acc_sc.shape, F32)

    def body(c, carry):
        off = pl.multiple_of(c * tk, tk)
        kt = kt_ref[0, :, pl.ds(off, tk)]
        vv = v_ref[pl.ds(off, tk), :]
        scores = [_dot(q_ref[:, g * HEAD_DIM:(g + 1) * HEAD_DIM], kt) for g in range(ATTN_GROUP)]
        for g in range(ATTN_GROUP):
            s = scores[g]
            tiles = [s[:, j * LANES:(j + 1) * LANES] for j in range(n_tiles)]
            mx = tiles[0]
            for t in tiles[1:]:
                mx = jnp.maximum(mx, t)
            m_prev = m_sc[g]
            m_new = jnp.maximum(m_prev, jnp.max(mx, axis=-1, keepdims=True))
            alpha = jnp.exp2(m_prev - m_new)
            ps = [jnp.exp2(t - m_new) for t in tiles]
            rs = ps[0]
            for p in ps[1:]:
                rs = rs + p
            l_sc[g] = alpha * l_sc[g] + rs
            p_all = jnp.concatenate([p.astype(BF16) for p in ps], axis=1)
            acc_sc[g] = alpha * acc_sc[g] + _dot(p_all, vv)
            m_sc[g] = m_new
        return carry

    lax.fori_loop(0, n_keys // tk, body, 0)
    for g in range(ATTN_GROUP):
        out = acc_sc[g] / jnp.sum(l_sc[g], axis=-1, keepdims=True)
        o_ref[:, g * HEAD_DIM:(g + 1) * HEAD_DIM] = out.astype(o_ref.dtype)


def _flash(q, kt, v, rows, n_keys, q_tiles, o_prev=None):
    n = q.shape[0]
    tq = rows.tile
    pb = rows.per_batch
    width = ATTN_GROUP * HEAD_DIM
    tk = n_keys
    for cand in (768, 512, 384, 256, 128):
        if n_keys % cand == 0:
            tk = cand
            break
    key_blocks = rows.seq // n_keys
    in_specs = [pl.BlockSpec((tq, width), lambda b, h, i: (b * pb + i, h)),
                pl.BlockSpec((1, HEAD_DIM, n_keys), lambda b, h, i: (b, h, 0)),
                pl.BlockSpec((n_keys, HEAD_DIM), lambda b, h, i: (b * key_blocks, h))]
    args = [q, kt, v]
    aliases = {}
    if o_prev is not None:
        in_specs.append(pl.BlockSpec(memory_space=pl.ANY))
        args.append(o_prev)
        aliases = {3: 0}

    def kernel(q_ref, kt_ref, v_ref, *rest):
        _flash_kernel(q_ref, kt_ref, v_ref, *rest[-4:], tk=tk)

    return pl.pallas_call(
        kernel,
        out_shape=jax.ShapeDtypeStruct((n, ATTN_HEADS * HEAD_DIM), BF16),
        grid=(rows.batch, ATTN_KV_HEADS, q_tiles),
        in_specs=in_specs,
        out_specs=pl.BlockSpec((tq, width), lambda b, h, i: (b * pb + i, h)),
        scratch_shapes=[pltpu.VMEM((ATTN_GROUP, tq, LANES), F32),
                        pltpu.VMEM((ATTN_GROUP, tq, LANES), F32),
                        pltpu.VMEM((ATTN_GROUP, tq, HEAD_DIM), F32)],
        input_output_aliases=aliases,
        compiler_params=_params("parallel", "parallel", "arbitrary"),
        name="flash_attention",
    )(*args)


def _attention_layer(h, mods, rows, cos, sin, norm_g, w_qkv, q_gain, k_gain, w_o, need_ctx):
    u = _norm_mod(h, norm_g, mods, rows, 0, BF16)
    q, kt, v = _qkv_project(u, w_qkv.astype(BF16), cos, sin, q_gain, k_gain, rows)
    o = _flash(q, kt, v, rows, rows.seq, rows.per_batch)
    if need_ctx:
        o = _flash(q, kt, v, rows, rows.ctx, rows.ctx_tiles, o_prev=o)
    return _matmul_residual(o, w_o.astype(BF16), h, mods, rows, 2)


def _router_kernel(h_ref, g_ref, shift_ref, scale_ref, w_ref, b_ref, t_ref, route_ref):
    t = _rms(h_ref[...]) * g_ref[...]
    t = t * (1.0 + scale_ref[0]) + shift_ref[0]
    t_ref[...] = t
    t_hi, t_lo = _split_bf16(t)
    w_hi, w_lo = _split_bf16(w_ref[...])
    logits = _dot(t_hi, w_hi) + _dot(t_lo, w_hi) + _dot(t_hi, w_lo) + b_ref[...]
    lane = lax.broadcasted_iota(jnp.int32, logits.shape, 1)
    neg = -jnp.inf
    is_group = (lane >= MOE_EXPERTS) & (lane < MOE_EXPERTS + MOE_GROUPS)
    gl = jnp.where(is_group, logits, neg)
    g_max = jnp.max(gl, axis=-1, keepdims=True)
    g_idx = jnp.min(jnp.where(gl == g_max, lane, LANES), axis=-1, keepdims=True) - MOE_EXPERTS
    g_w = 1.0 / jnp.sum(jnp.exp(gl - g_max), axis=-1, keepdims=True)
    in_group = (lane >= g_idx * MOE_PER_GROUP) & (lane < (g_idx + 1) * MOE_PER_GROUP)
    el = jnp.where(in_group, logits, neg)
    m1 = jnp.max(el, axis=-1, keepdims=True)
    i1 = jnp.min(jnp.where(el == m1, lane, LANES), axis=-1, keepdims=True)
    el2 = jnp.where(lane == i1, neg, el)
    m2 = jnp.max(el2, axis=-1, keepdims=True)
    i2 = jnp.min(jnp.where(el2 == m2, lane, LANES), axis=-1, keepdims=True)
    e2 = jnp.exp(m2 - m1)
    w1 = 1.0 / (1.0 + e2)
    w2 = e2 * w1
    route = jnp.where(lane == 0, i1.astype(F32), 0.0) + jnp.where(lane == 1, i2.astype(F32), 0.0)
    route_ref[...] = route + jnp.where(lane == 2, g_w * w1, 0.0) + jnp.where(lane == 3, g_w * w2, 0.0)


def _router(h, norm_g, mods, rows, w_rt, b_rt):
    n, d = h.shape
    tm = rows.tile
    return pl.pallas_call(
        _router_kernel,
        out_shape=(jax.ShapeDtypeStruct((n, d), F32), jax.ShapeDtypeStruct((n, LANES), F32)),
        grid=(rows.n_tiles,),
        in_specs=[pl.BlockSpec((tm, d), lambda i: (i, 0)),
                  pl.BlockSpec((1, d), lambda i: (0, 0)),
                  pl.BlockSpec((1, 1, d), lambda i: (rows.mod_index(i, 3), 0, 0)),
                  pl.BlockSpec((1, 1, d), lambda i: (rows.mod_index(i, 4), 0, 0)),
                  pl.BlockSpec((d, LANES), lambda i: (0, 0)),
                  pl.BlockSpec((1, LANES), lambda i: (0, 0))],
        out_specs=(pl.BlockSpec((tm, d), lambda i: (i, 0)),
                   pl.BlockSpec((tm, LANES), lambda i: (i, 0))),
        compiler_params=_params("parallel"),
        name="moe_router",
    )(h, norm_g.reshape(1, d), mods, mods, w_rt, b_rt)


def _rank_kernel(route_ref, idx_ref, cnt_ref, carry_sc):
    @pl.when(pl.program_id(0) == 0)
    def _():
        carry_sc[...] = jnp.zeros(carry_sc.shape, F32)

    route = route_ref[...]
    tm = route.shape[0]
    lane = lax.broadcasted_iota(jnp.int32, route.shape, 1)
    lane_f = lane.astype(F32)
    e1, e2 = route[:, 0:1], route[:, 1:2]
    hit1, hit2 = lane_f == e1, lane_f == e2
    onehot = jnp.where(hit1 | hit2, 1.0, 0.0)
    row = lax.broadcasted_iota(jnp.int32, (tm, tm), 0)
    col = lax.broadcasted_iota(jnp.int32, (tm, tm), 1)
    earlier = jnp.where(col < row, 1.0, 0.0).astype(BF16)
    before = _dot(earlier, onehot.astype(BF16)) + carry_sc[...]
    r1 = jnp.sum(jnp.where(hit1, before, 0.0), axis=-1, keepdims=True)
    r2 = jnp.sum(jnp.where(hit2, before, 0.0), axis=-1, keepdims=True)
    carry_sc[...] += jnp.sum(onehot, axis=0, keepdims=True)
    packed = (jnp.where(lane == 0, e1, 0.0) + jnp.where(lane == 1, e2, 0.0)
              + jnp.where(lane == 2, r1, 0.0) + jnp.where(lane == 3, r2, 0.0))
    idx_ref[0] = packed.T[0:8, :].astype(jnp.int32)
    cnt_ref[...] = carry_sc[...]


def _rank(route, tm):
    n = route.shape[0]
    return pl.pallas_call(
        _rank_kernel,
        out_shape=(jax.ShapeDtypeStruct((n // tm, 8, tm), jnp.int32), jax.ShapeDtypeStruct((1, LANES), F32)),
        grid=(n // tm,),
        in_specs=[pl.BlockSpec((tm, LANES), lambda i: (i, 0))],
        out_specs=(pl.BlockSpec((1, 8, tm), lambda i: (i, 0, 0)),
                   pl.BlockSpec((1, LANES), lambda i: (0, 0))),
        scratch_shapes=[pltpu.VMEM((1, LANES), F32)],
        compiler_params=_params("arbitrary"),
        name="moe_rank",
    )(route)


def _dest_row(offs_ref, idx_ref, slot, r):
    return offs_ref[idx_ref[0, slot, r]] + idx_ref[0, 2 + slot, r]


def _dispatch_kernel(offs_ref, idx_ref, t_ref, xs_in_ref, xs_ref, sem):
    del xs_in_ref
    tm = t_ref.shape[0]

    def row_copy(r, dst):
        return pltpu.make_async_copy(t_ref.at[pl.ds(r, 1), :], xs_ref.at[pl.ds(dst, 1), :], sem)

    def start(r, c):
        row_copy(r, _dest_row(offs_ref, idx_ref, 0, r)).start()
        row_copy(r, _dest_row(offs_ref, idx_ref, 1, r)).start()
        return c

    def wait(r, c):
        row_copy(0, 0).wait()
        return c

    lax.fori_loop(0, tm, start, 0, unroll=8)
    lax.fori_loop(0, 2 * tm, wait, 0, unroll=8)


def _dispatch(t, idx, offs, n_sorted, tm):
    n, d = t.shape
    return pl.pallas_call(
        _dispatch_kernel,
        out_shape=jax.ShapeDtypeStruct((n_sorted, d), F32),
        grid_spec=pltpu.PrefetchScalarGridSpec(
            num_scalar_prefetch=1,
            grid=(n // tm,),
            in_specs=[pl.BlockSpec((1, 8, tm), lambda i, o: (i, 0, 0), memory_space=pltpu.SMEM),
                      pl.BlockSpec((tm, d), lambda i, o: (i, 0)),
                      pl.BlockSpec(memory_space=pl.ANY)],
            out_specs=pl.BlockSpec(memory_space=pl.ANY),
            scratch_shapes=[pltpu.SemaphoreType.DMA(())]),
        input_output_aliases={3: 0},
        compiler_params=_params("arbitrary"),
        name="moe_dispatch",
    )(offs, idx, t, jnp.zeros((n_sorted, d), F32))


def _grouped_kernel(tile_e_ref, used_ref, x_ref, wg_ref, wu_ref, wd_ref, y_ref):
    del tile_e_ref

    @pl.when(pl.program_id(0) < used_ref[0])
    def _():
        x = x_ref[...].astype(BF16)
        hid = _silu(_dot(x, wg_ref[0])) * _dot(x, wu_ref[0])
        y_ref[...] = _dot(hid.astype(BF16), wd_ref[0])

    @pl.when(pl.program_id(0) >= used_ref[0])
    def _():
        y_ref[...] = jnp.zeros(y_ref.shape, F32)


def _grouped_experts(xs, tile_e, n_used, w_gate, w_up, w_down, tm):
    n_sorted, d = xs.shape
    hid = w_gate.shape[2]
    row = lambda j, te, nu: (jnp.minimum(j, nu[0] - 1), 0)
    wsel = lambda j, te, nu: (te[j], 0, 0)
    return pl.pallas_call(
        _grouped_kernel,
        out_shape=jax.ShapeDtypeStruct((n_sorted, d), F32),
        grid_spec=pltpu.PrefetchScalarGridSpec(
            num_scalar_prefetch=2,
            grid=(n_sorted // tm,),
            in_specs=[pl.BlockSpec((tm, d), row),
                      pl.BlockSpec((1, d, hid), wsel),
                      pl.BlockSpec((1, d, hid), wsel),
                      pl.BlockSpec((1, hid, d), wsel)],
            out_specs=pl.BlockSpec((tm, d), lambda j, te, nu: (j, 0))),
        compiler_params=_params("arbitrary"),
        name="moe_experts",
    )(tile_e, n_used, xs, w_gate, w_up, w_down)


def _combine_kernel(offs_ref, idx_ref, route_ref, res_ref, gate_ref, ys_ref, o_ref, ya_sc, yb_sc, sem):
    tm = res_ref.shape[0]

    def row_copy(src, dst_sc, r):
        return pltpu.make_async_copy(ys_ref.at[pl.ds(src, 1), :], dst_sc.at[pl.ds(r, 1), :], sem)

    def start(r, c):
        row_copy(_dest_row(offs_ref, idx_ref, 0, r), ya_sc, r).start()
        row_copy(_dest_row(offs_ref, idx_ref, 1, r), yb_sc, r).start()
        return c

    def wait(r, c):
        row_copy(0, ya_sc, 0).wait()
        return c

    lax.fori_loop(0, tm, start, 0, unroll=8)
    lax.fori_loop(0, 2 * tm, wait, 0, unroll=8)
    route = route_ref[...]
    mix = route[:, 2:3] * ya_sc[...] + route[:, 3:4] * yb_sc[...]
    o_ref[...] = res_ref[...] + gate_ref[0] * mix


def _combine(ys, idx, offs, route, res, mods, rows):
    n, d = res.shape
    tm = rows.tile
    return pl.pallas_call(
        _combine_kernel,
        out_shape=jax.ShapeDtypeStruct((n, d), F32),
        grid_spec=pltpu.PrefetchScalarGridSpec(
            num_scalar_prefetch=1,
            grid=(rows.n_tiles,),
            in_specs=[pl.BlockSpec((1, 8, tm), lambda i, o: (i, 0, 0), memory_space=pltpu.SMEM),
                      pl.BlockSpec((tm, LANES), lambda i, o: (i, 0)),
                      pl.BlockSpec((tm, d), lambda i, o: (i, 0)),
                      pl.BlockSpec((1, 1, d), lambda i, o: (rows.mod_index(i, 5), 0, 0)),
                      pl.BlockSpec(memory_space=pl.ANY)],
            out_specs=pl.BlockSpec((tm, d), lambda i, o: (i, 0)),
            scratch_shapes=[pltpu.VMEM((tm, d), F32), pltpu.VMEM((tm, d), F32),
                            pltpu.SemaphoreType.DMA(())]),
        compiler_params=_params("arbitrary"),
        name="moe_combine",
    )(offs, idx, route, res, mods, ys)


def _moe_layer(h, mods, rows, norm_g, w_group, b_group, w_router, b_router, w_gate, w_up, w_down):
    n, d = h.shape
    pad = LANES - MOE_EXPERTS - MOE_GROUPS
    w_rt = jnp.concatenate([w_router, w_group, jnp.zeros((d, pad), F32)], axis=1)
    b_rt = jnp.concatenate([b_router, b_group, jnp.zeros((pad,), F32)]).reshape(1, LANES)
    t, route = _router(h, norm_g, mods, rows, w_rt, b_rt)
    idx, counts = _rank(route, rows.tile)

    tile = MOE_TILE
    max_tiles = -(-2 * n // tile) + MOE_EXPERTS
    cnt = counts[0, :MOE_EXPERTS].astype(jnp.int32)
    padded = (cnt + tile - 1) // tile * tile
    ends = jnp.cumsum(padded)
    offs = ends - padded
    n_used = (ends[-1] // tile).reshape(1)
    tile_start = jnp.minimum(jnp.arange(max_tiles, dtype=jnp.int32), n_used - 1) * tile
    tile_e = jnp.minimum(jnp.searchsorted(ends, tile_start, side="right"), MOE_EXPERTS - 1).astype(jnp.int32)

    xs = _dispatch(t, idx, offs, max_tiles * tile, rows.tile)
    ys = _grouped_experts(xs, tile_e, n_used, w_gate.astype(BF16), w_up.astype(BF16), w_down.astype(BF16), tile)
    return _combine(ys, idx, offs, route, h, mods, rows)


def _scan_block(step, n_ctx_blocks, n_blocks, rev):
    if not rev:
        return step
    return jnp.where(step < n_ctx_blocks, n_ctx_blocks - 1 - step, 2 * n_ctx_blocks + n_blocks - 1 - step)


def _s5_disc_kernel(a_re_ref, a_im_ref, ldt_ref, b_re_ref, b_im_ref, abr_ref, abi_ref, bbr_ref, bbi_ref):
    a_re, a_im = a_re_ref[...], a_im_ref[...]
    dt = jnp.exp(ldt_ref[...])
    mag = jnp.exp(a_re * dt)
    ab_re = mag * jnp.cos(a_im * dt)
    ab_im = mag * jnp.sin(a_im * dt)
    den = a_re * a_re + a_im * a_im
    num_re = ab_re - 1.0
    coef_re = (num_re * a_re + ab_im * a_im) / den
    coef_im = (ab_im * a_re - num_re * a_im) / den
    abr_ref[...] = ab_re
    abi_ref[...] = ab_im
    for c in range(S5_GROUP):
        b_re, b_im = b_re_ref[c], b_im_ref[c]
        bbr_ref[c] = coef_re * b_re - coef_im * b_im
        bbi_ref[c] = coef_re * b_im + coef_im * b_re


def _s5_discretise(a_re, a_im, log_dt, b_re, b_im):
    nd, g, p = a_re.shape
    n = nd * g
    bt = lambda b: jnp.transpose(b, (3, 0, 1, 2)).reshape(S5_GROUP, n, p)
    shp = jax.ShapeDtypeStruct((n, p), F32)
    shp_b = jax.ShapeDtypeStruct((S5_GROUP, n, p), F32)
    return pl.pallas_call(_s5_disc_kernel, out_shape=(shp, shp, shp_b, shp_b), name="s5_discretise")(
        a_re.reshape(n, p), a_im.reshape(n, p), log_dt.reshape(n, 1), bt(b_re), bt(b_im))


def _s5_scan_kernel(u_ref, bm_ref, cm_ref, ar_ref, ai_ref, y_ref, bu_sc, st_sc, *, nb, steps, rev):
    @pl.when(pl.program_id(0) == 0)
    def _():
        st_sc[...] = jnp.zeros(st_sc.shape, F32)

    n_slabs = u_ref.shape[1] // LANES
    ns = S5_SLAB_STATE
    for s in range(n_slabs):
        us = u_ref[:, s * LANES:(s + 1) * LANES].astype(BF16)
        bu_sc[...] = _dot(us, bm_ref[s])
        ar = jnp.broadcast_to(ar_ref[s], (nb, ns))
        ai = jnp.broadcast_to(ai_ref[s], (nb, ns))

        def step(k, carry):
            xr, xi = carry
            t = (steps - 1 - k) if rev else k
            r0 = pl.multiple_of(t * nb, nb)
            br = bu_sc[pl.ds(r0, nb), 0:ns]
            bi = bu_sc[pl.ds(r0, nb), ns:2 * ns]
            nr = ar * xr - ai * xi + br
            ni = ar * xi + ai * xr + bi
            bu_sc[pl.ds(r0, nb), 0:ns] = nr
            bu_sc[pl.ds(r0, nb), ns:2 * ns] = ni
            return nr, ni

        xr, xi = lax.fori_loop(0, steps, step, (st_sc[s, 0], st_sc[s, 1]), unroll=8)
        st_sc[s, 0] = xr
        st_sc[s, 1] = xi
        y_ref[:, s * LANES:(s + 1) * LANES] = _dot(bu_sc[...].astype(BF16), cm_ref[s])


def _s5_scan(u_tm, bmat, cmat, ab_re, ab_im, nb, n_ctx, rev):
    n, d = u_tm.shape
    steps = S5_STEPS
    blk = steps * nb
    n_blocks = n // blk
    n_ctx_blocks = n_ctx // steps
    n_slabs = d // LANES
    ns = S5_SLAB_STATE
    idx = lambda s: (_scan_block(s, n_ctx_blocks, n_blocks - n_ctx_blocks, rev), 0)
    return pl.pallas_call(
        functools.partial(_s5_scan_kernel, nb=nb, steps=steps, rev=rev),
        out_shape=jax.ShapeDtypeStruct((n, d), F32),
        grid=(n_blocks,),
        in_specs=[pl.BlockSpec((blk, d), idx),
                  pl.BlockSpec((n_slabs, LANES, 2 * ns), lambda s: (0, 0, 0)),
                  pl.BlockSpec((n_slabs, 2 * ns, LANES), lambda s: (0, 0, 0)),
                  pl.BlockSpec((n_slabs, 1, ns), lambda s: (0, 0, 0)),
                  pl.BlockSpec((n_slabs, 1, ns), lambda s: (0, 0, 0))],
        out_specs=pl.BlockSpec((blk, d), idx),
        scratch_shapes=[pltpu.VMEM((blk, 2 * ns), F32),
                        pltpu.VMEM((n_slabs, 2, nb, ns), F32)],
        compiler_params=_params("arbitrary"),
        name="s5_scan_bwd" if rev else "s5_scan_fwd",
    )(u_tm, bmat, cmat, ab_re, ab_im)


def _gelu_tanh(x):
    return 0.5 * x * (1.0 + jnp.tanh(math.sqrt(2.0 / math.pi) * (x + 0.044715 * (x * x * x))))


def _s5_glu_kernel(yf_ref, yb_ref, u_ref, d_ref, wa_ref, wb_ref, ba_ref, bb_ref, o_ref):
    y = yf_ref[...] + yb_ref[...] + u_ref[...] * d_ref[...]
    z = _gelu_tanh(y).astype(BF16)
    a = _dot(z, wa_ref[...]) + ba_ref[...]
    b = _dot(z, wb_ref[...]) + bb_ref[...]
    o_ref[...] = a * _sigmoid(b)


def _s5_glu(y_f, y_b, u, d_skip, w_glu, b_glu, tm):
    n, d = u.shape
    row = pl.BlockSpec((tm, d), lambda i: (i, 0))
    vec = pl.BlockSpec((1, d), lambda i: (0, 0))
    return pl.pallas_call(
        _s5_glu_kernel,
        out_shape=jax.ShapeDtypeStruct((n, d), F32),
        grid=(n // tm,),
        in_specs=[row, row, row, vec,
                  pl.BlockSpec((d, d), lambda i: (0, 0)),
                  pl.BlockSpec((d, d), lambda i: (0, 1)),
                  vec,
                  pl.BlockSpec((1, d), lambda i: (0, 1))],
        out_specs=row,
        compiler_params=_params("parallel"),
        name="s5_glu",
    )(y_f, y_b, u, d_skip.reshape(1, d), w_glu, w_glu, b_glu.reshape(1, 2 * d), b_glu.reshape(1, 2 * d))


def _residual_kernel(h_ref, y_ref, gate_ref, o_ref):
    o_ref[...] = h_ref[...] + gate_ref[0] * y_ref[...]


def _residual(h, y, mods, rows, gate_row):
    n, d = h.shape
    tm = rows.tile
    row = pl.BlockSpec((tm, d), lambda i: (i, 0))
    return pl.pallas_call(
        _residual_kernel,
        out_shape=jax.ShapeDtypeStruct((n, d), F32),
        grid=(rows.n_tiles,),
        in_specs=[row, row, pl.BlockSpec((1, 1, d), lambda i: (rows.mod_index(i, gate_row), 0, 0))],
        out_specs=row,
        compiler_params=_params("parallel"),
        name="residual",
    )(h, y, mods)


def _s5_layer(h, mods, rows, norm_g, a_re, a_im, log_dt, b_re, b_im, c_re, c_im, d_skip, w_glu, b_glu):
    nb, seq = rows.batch, rows.seq
    d = h.shape[1]
    n_slabs = d // LANES
    sg = S5_SLAB_GROUPS
    ab_re, ab_im, bb_re, bb_im = _s5_discretise(a_re, a_im, log_dt, b_re, b_im)
    eye = jnp.eye(sg, dtype=F32)

    def b_blocks(bb):
        t = bb.reshape(S5_GROUP, 2, n_slabs, sg, S5_STATE)
        t = jnp.einsum("cdsgp,gh->dsgchp", t, eye)
        return t.reshape(2, n_slabs, LANES, S5_SLAB_STATE)

    def c_blocks(cc):
        t = cc.reshape(2, n_slabs, sg, S5_GROUP, S5_STATE)
        t = jnp.einsum("dsgcp,gh->dshpgc", t, eye)
        return t.reshape(2, n_slabs, S5_SLAB_STATE, LANES)

    bmat = jnp.concatenate([b_blocks(bb_re), b_blocks(bb_im)], axis=-1).astype(BF16)
    cmat = jnp.concatenate([c_blocks(c_re), -c_blocks(c_im)], axis=2).astype(BF16)
    ab_re = ab_re.reshape(2, n_slabs, 1, S5_SLAB_STATE)
    ab_im = ab_im.reshape(2, n_slabs, 1, S5_SLAB_STATE)

    u = _norm_mod(h, norm_g, mods, rows, 0, F32)
    u_tm = jnp.transpose(u.reshape(nb, seq, d), (1, 0, 2)).reshape(seq * nb, d)
    y_f = _s5_scan(u_tm, bmat[0], cmat[0], ab_re[0], ab_im[0], nb, rows.ctx, False)
    y_b = _s5_scan(u_tm, bmat[1], cmat[1], ab_re[1], ab_im[1], nb, rows.ctx, True)
    mix_tm = _s5_glu(y_f, y_b, u_tm, d_skip, w_glu.astype(BF16), b_glu, S5_STEPS * nb)
    mix = jnp.transpose(mix_tm.reshape(seq, nb, d), (1, 0, 2)).reshape(nb * seq, d)
    return _residual(h, mix, mods, rows, 2)


def _ssd_conv_kernel(x_ref, w_ref, b_ref, o_ref, pad_sc, *, ctx, n_lat, chunk):
    tc = x_ref.shape[2]
    zeros = jnp.zeros((8, tc), F32)
    lat0 = 16 + ctx
    pad_sc[0:8, :] = zeros
    pad_sc[8 + ctx:lat0, :] = zeros
    pad_sc[lat0 + n_lat:lat0 + n_lat + 8, :] = zeros
    half = SSD_CONV // 2
    segments = ((0, ctx, 8), (ctx, n_lat, lat0))
    for src0, length, dst0 in segments:
        for r in range(0, length, chunk):
            pad_sc[dst0 + r:dst0 + r + chunk, :] = x_ref[0, src0 + r:src0 + r + chunk, :]
    w = w_ref[...]
    for src0, length, dst0 in segments:
        for r in range(0, length, chunk):
            acc = jnp.broadcast_to(b_ref[...], (chunk, tc))
            for k in range(SSD_CONV):
                lo = dst0 + r + k - half
                acc = acc + w[k:k + 1, :] * pad_sc[lo:lo + chunk, :]
            o_ref[0, src0 + r:src0 + r + chunk, :] = _silu(acc)


def _ssd_conv(xbc, conv_w, conv_b, ctx):
    nb, seq, ch = xbc.shape
    tc = LANES
    n_lat = seq - ctx
    return pl.pallas_call(
        functools.partial(_ssd_conv_kernel, ctx=ctx, n_lat=n_lat, chunk=ROW_TILE),
        out_shape=jax.ShapeDtypeStruct((nb, seq, ch), F32),
        grid=(nb, ch // tc),
        in_specs=[pl.BlockSpec((1, seq, tc), lambda b, j: (b, 0, j)),
                  pl.BlockSpec((SSD_CONV, tc), lambda b, j: (0, j)),
                  pl.BlockSpec((1, tc), lambda b, j: (0, j))],
        out_specs=pl.BlockSpec((1, seq, tc), lambda b, j: (b, 0, j)),
        scratch_shapes=[pltpu.VMEM((seq + 24, tc), F32)],
        compiler_params=_params("parallel", "parallel"),
        name="ssd_conv",
    )(xbc, conv_w, conv_b.reshape(1, ch))


def _softplus(x):
    return jnp.maximum(x, 0.0) + jnp.log(1.0 + jnp.exp(-jnp.abs(x)))


def _ssd_scan_kernel(xbc_ref, dt_ref, dtb_ref, a_ref, y_ref, st_sc, *, rev, col0, n_heads):
    @pl.when(pl.program_id(1) == 0)
    def _():
        st_sc[...] = jnp.zeros(st_sc.shape, F32)

    q = xbc_ref.shape[1]
    hpg = n_heads // SSD_GROUPS
    inner = n_heads * SSD_HEAD_DIM
    dt = _softplus(dt_ref[0] + dtb_ref[...])
    a = dt * a_ref[...]
    row = lax.broadcasted_iota(jnp.int32, (q, q), 0)
    col = lax.broadcasted_iota(jnp.int32, (q, q), 1)
    tri = (col >= row) if rev else (col <= row)
    tri_b = jnp.where(tri, 1.0, 0.0).astype(BF16)
    a_hi, a_lo = _split_bf16(a)
    cum = _dot(tri_b, a_hi) + _dot(tri_b, a_lo)
    cum_t = cum.T
    total = cum[0:1, :] if rev else cum[q - 1:q, :]
    for g in range(SSD_GROUPS):
        bg = xbc_ref[0, :, inner + g * SSD_STATE:inner + (g + 1) * SSD_STATE]
        cg = xbc_ref[0, :, inner + (SSD_GROUPS + g) * SSD_STATE:inner + (SSD_GROUPS + g + 1) * SSD_STATE]
        cg_b = cg.astype(BF16)
        cb = lax.dot_general(cg_b, bg.astype(BF16), (((1,), (1,)), ((), ())), preferred_element_type=F32)
        bg_t = bg.T
        for hh in range(hpg):
            hd = g * hpg + hh
            c = col0 + hd
            cum_c = cum[:, c:c + 1]
            cum_r = cum_t[c:c + 1, :]
            decay = jnp.exp(jnp.where(tri, cum_c - cum_r, -jnp.inf))
            xdt = (xbc_ref[0, :, hd * SSD_HEAD_DIM:(hd + 1) * SSD_HEAD_DIM] * dt[:, c:c + 1]).astype(BF16)
            state = st_sc[hd]
            y = _dot((cb * decay).astype(BF16), xdt) + jnp.exp(cum_c) * _dot(cg_b, state.astype(BF16))
            y_ref[0, :, hd * SSD_HEAD_DIM:(hd + 1) * SSD_HEAD_DIM] = y
            tot = total[:, c:c + 1]
            tail = jnp.exp(tot - cum_r)
            st_sc[hd] = jnp.exp(tot) * state + _dot((bg_t * tail).astype(BF16), xdt)


def _ssd_scan(xbc, dt_raw, dt_bias, a_neg, ctx, rev, n_heads):
    nb, seq, ch = xbc.shape
    q = SSD_CHUNK
    n_blocks = seq // q
    n_ctx_blocks = ctx // q
    inner = n_heads * SSD_HEAD_DIM
    blk = lambda b, s: (b, _scan_block(s, n_ctx_blocks, n_blocks - n_ctx_blocks, rev), 0)
    return pl.pallas_call(
        functools.partial(_ssd_scan_kernel, rev=rev, col0=n_heads if rev else 0, n_heads=n_heads),
        out_shape=jax.ShapeDtypeStruct((nb, seq, inner), F32),
        grid=(nb, n_blocks),
        in_specs=[pl.BlockSpec((1, q, ch), blk),
                  pl.BlockSpec((1, q, LANES), blk),
                  pl.BlockSpec((1, LANES), lambda b, s: (0, 0)),
                  pl.BlockSpec((1, LANES), lambda b, s: (0, 0))],
        out_specs=pl.BlockSpec((1, q, inner), blk),
        scratch_shapes=[pltpu.VMEM((n_heads, SSD_STATE, SSD_HEAD_DIM), F32)],
        compiler_params=_params("parallel", "arbitrary"),
        name="ssd_scan_bwd" if rev else "ssd_scan_fwd",
    )(xbc, dt_raw, dt_bias, a_neg)


def _ssd_finish_kernel(yf_ref, yb_ref, x_ref, z_ref, d_ref, g_ref, o_ref):
    y = yf_ref[...] + yb_ref[...] + x_ref[...] * d_ref[...]
    o_ref[...] = (_rms(y * _silu(z_ref[...])) * g_ref[...]).astype(o_ref.dtype)


def _ssd_finish(y_f, y_b, xbc, z, d_skip, norm_g, tm):
    n, inner = z.shape
    row = pl.BlockSpec((tm, inner), lambda i: (i, 0))
    vec = pl.BlockSpec((1, inner), lambda i: (0, 0))
    return pl.pallas_call(
        _ssd_finish_kernel,
        out_shape=jax.ShapeDtypeStruct((n, inner), BF16),
        grid=(n // tm,),
        in_specs=[row, row, row, row, vec, vec],
        out_specs=row,
        compiler_params=_params("parallel"),
        name="ssd_finish",
    )(y_f, y_b, xbc, z, d_skip, norm_g)


def _ssd_layer(h, mods, rows, norm_g, w_in, conv_w, conv_b, dt_bias, a_log, d_skip, ssd_norm_g, w_out):
    nb, seq = rows.batch, rows.seq
    n = nb * seq
    n_heads = dt_bias.shape[1]
    inner = n_heads * SSD_HEAD_DIM
    conv_ch = inner + 2 * SSD_GROUPS * SSD_STATE
    u = _norm_mod(h, norm_g, mods, rows, 0, BF16)
    w_in = w_in.astype(BF16)
    d = w_in.shape[0]
    z = _matmul(u, w_in[:, :inner], F32, tn=1024)
    xbc = _matmul(u, w_in[:, inner:inner + conv_ch], F32, tn=1024)
    w_dt = jnp.concatenate([w_in[:, inner + conv_ch:], jnp.zeros((d, LANES - 2 * n_heads), BF16)], axis=1)
    dt_raw = _matmul(u, w_dt, F32)
    xbc = _ssd_conv(xbc.reshape(nb, seq, conv_ch), conv_w, conv_b, rows.ctx)
    pad = jnp.zeros((LANES - 2 * n_heads,), F32)
    dtb = jnp.concatenate([dt_bias.reshape(-1), pad]).reshape(1, LANES)
    a_neg = jnp.concatenate([a_log.reshape(-1), pad]).reshape(1, LANES)
    a_neg = _neg_exp(a_neg, 2 * n_heads)
    dt3 = dt_raw.reshape(nb, seq, LANES)
    y_f = _ssd_scan(xbc, dt3, dtb, a_neg, rows.ctx, False, n_heads)
    y_b = _ssd_scan(xbc, dt3, dtb, a_neg, rows.ctx, True, n_heads)
    dsk = jnp.repeat(d_skip, SSD_HEAD_DIM).reshape(1, inner)
    g = _ssd_finish(y_f.reshape(n, inner), y_b.reshape(n, inner), xbc.reshape(n, conv_ch), z, dsk,
                    ssd_norm_g.reshape(1, inner), rows.tile)
    return _matmul_residual(g, w_out.astype(BF16), h, mods, rows, 2)


def _neg_exp_kernel(x_ref, o_ref, *, valid):
    lane = lax.broadcasted_iota(jnp.int32, x_ref.shape, 1)
    o_ref[...] = jnp.where(lane < valid, -jnp.exp(x_ref[...]), 0.0)


def _neg_exp(x, valid):
    return pl.pallas_call(functools.partial(_neg_exp_kernel, valid=valid),
                          out_shape=jax.ShapeDtypeStruct(x.shape, F32), name="ssd_decay_rate")(x)


def kernel(x, c, ctx, c_ctx, mod_w, mod_b, norm1_g, norm2_g, attn_w_qkv, attn_q_gain, attn_k_gain, attn_w_o, s5_a_re, s5_a_im, s5_log_dt, s5_b_re, s5_b_im, s5_c_re, s5_c_im, s5_d, s5_w_glu, s5_b_glu, ssd_w_in, ssd_conv_w, ssd_conv_b, ssd_dt_bias, ssd_a_log, ssd_d, ssd_norm_g, ssd_w_out, moe_w_group, moe_b_group, moe_w_router, moe_b_router, moe_w_gate, moe_w_up, moe_w_down):
    batch, n_lat, d = x.shape
    n_ctx = ctx.shape[1]
    depth = mod_w.shape[0]
    seq = n_ctx + n_lat
    rows = _Rows(batch, seq, n_ctx, ROW_TILE)

    n_cond = -(-(batch + 1) // 8) * 8
    cond = jnp.concatenate([c, c_ctx[None, :], jnp.zeros((n_cond - batch - 1, d), F32)], axis=0)
    mods_all = _modulation(cond, mod_w, mod_b).reshape(depth, n_cond * 6, 1, d)

    cos, sin = _rope_tables(n_lat, n_ctx)
    h = jnp.concatenate([ctx, x], axis=1).reshape(batch * seq, d)

    for i in range(depth):
        need_ctx = i < depth - 1
        kind, j = i % N_MIXERS, i // N_MIXERS
        mods = mods_all[i]
        if kind == 0:
            h = _attention_layer(h, mods, rows, cos, sin, norm1_g[i], attn_w_qkv[j], attn_q_gain[j],
                                 attn_k_gain[j], attn_w_o[j], need_ctx)
        elif kind == 1:
            h = _s5_layer(h, mods, rows, norm1_g[i], s5_a_re[j], s5_a_im[j], s5_log_dt[j], s5_b_re[j], s5_b_im[j],
                          s5_c_re[j], s5_c_im[j], s5_d[j], s5_w_glu[j], s5_b_glu[j])
        else:
            h = _ssd_layer(h, mods, rows, norm1_g[i], ssd_w_in[j], ssd_conv_w[j], ssd_conv_b[j], ssd_dt_bias[j],
                           ssd_a_log[j], ssd_d[j], ssd_norm_g[j], ssd_w_out[j])
        h = _moe_layer(h, mods, rows, norm2_g[i], moe_w_group[i], moe_b_group[i], moe_w_router[i],
                       moe_b_router[i], moe_w_gate[i], moe_w_up[i], moe_w_down[i])
    return h.reshape(batch, seq, d)[:, n_ctx:, :]
```

```python
import functools
import math

import jax
import jax.numpy as jnp
from jax import lax
from jax.experimental import pallas as pl
from jax.experimental.pallas import tpu as pltpu

F32 = jnp.float32
BF16 = jnp.bfloat16

NORM_EPS = 1e-6
GRID_W = 64
N_MIXERS = 3

ATTN_HEADS = 8
ATTN_KV_HEADS = 2
ATTN_GROUP = ATTN_HEADS // ATTN_KV_HEADS
HEAD_DIM = 128
ROPE_THETA = 10000.0

S5_GROUP = 16
S5_STATE = 64
S5_SLAB_GROUPS = 8
S5_SLAB_STATE = S5_SLAB_GROUPS * S5_STATE
S5_STEPS = 64

SSD_HEAD_DIM = 64
SSD_GROUPS = 4
SSD_STATE = 128
SSD_CONV = 5
SSD_CHUNK = 128

MOE_GROUPS = 4
MOE_PER_GROUP = 8
MOE_EXPERTS = MOE_GROUPS * MOE_PER_GROUP
MOE_HIDDEN = 256
MOE_TILE = 512

LANES = 128
ROW_TILE = 256
VMEM_LIMIT = 56 * 1024 * 1024


def _params(*sem):
    return pltpu.CompilerParams(dimension_semantics=sem, vmem_limit_bytes=VMEM_LIMIT)


def _rms(x):
    return x * lax.rsqrt(jnp.mean(x * x, axis=-1, keepdims=True) + NORM_EPS)


def _silu(x):
    return x * (1.0 / (1.0 + jnp.exp(-x)))


def _sigmoid(x):
    return 1.0 / (1.0 + jnp.exp(-x))


def _dot(a, b):
    return jnp.dot(a, b, preferred_element_type=F32)


def _split_bf16(x):
    hi = x.astype(BF16)
    lo = (x - hi.astype(F32)).astype(BF16)
    return hi, lo


class _Rows:
    def __init__(self, batch, seq, ctx, tile):
        assert seq % tile == 0 and ctx % tile == 0
        self.batch, self.seq, self.ctx, self.tile = batch, seq, ctx, tile
        self.per_batch = seq // tile
        self.ctx_tiles = ctx // tile
        self.n_tiles = batch * self.per_batch

    def mod_index(self, i, row):
        b = i // self.per_batch
        is_ctx = (i % self.per_batch) < self.ctx_tiles
        return jnp.where(is_ctx, self.batch, b) * 6 + row


def _mod_kernel(c_ref, w_ref, b_ref, o_ref):
    a = _silu(c_ref[...]).astype(BF16)
    o_ref[0] = _dot(a, w_ref[0].astype(BF16)) + b_ref[0]


def _modulation(cond, mod_w, mod_b):
    depth, d, n = mod_w.shape
    rows = cond.shape[0]
    tn = 1536
    return pl.pallas_call(
        _mod_kernel,
        out_shape=jax.ShapeDtypeStruct((depth, rows, n), F32),
        grid=(depth, n // tn),
        in_specs=[pl.BlockSpec((rows, d), lambda l, j: (0, 0)),
                  pl.BlockSpec((1, d, tn), lambda l, j: (l, 0, j)),
                  pl.BlockSpec((1, 1, tn), lambda l, j: (l, 0, j))],
        out_specs=pl.BlockSpec((1, rows, tn), lambda l, j: (l, 0, j)),
        compiler_params=_params("parallel", "parallel"),
        name="modulation",
    )(cond, mod_w, mod_b.reshape(depth, 1, n))


def _norm_mod_kernel(h_ref, g_ref, shift_ref, scale_ref, o_ref):
    u = _rms(h_ref[...]) * g_ref[...]
    o_ref[...] = (u * (1.0 + scale_ref[0]) + shift_ref[0]).astype(o_ref.dtype)


def _norm_mod(h, g, mods, rows, shift_row, out_dtype):
    n, d = h.shape
    tm = rows.tile
    return pl.pallas_call(
        _norm_mod_kernel,
        out_shape=jax.ShapeDtypeStruct((n, d), out_dtype),
        grid=(rows.n_tiles,),
        in_specs=[pl.BlockSpec((tm, d), lambda i: (i, 0)),
                  pl.BlockSpec((1, d), lambda i: (0, 0)),
                  pl.BlockSpec((1, 1, d), lambda i: (rows.mod_index(i, shift_row), 0, 0)),
                  pl.BlockSpec((1, 1, d), lambda i: (rows.mod_index(i, shift_row + 1), 0, 0))],
        out_specs=pl.BlockSpec((tm, d), lambda i: (i, 0)),
        compiler_params=_params("parallel"),
        name="norm_mod",
    )(h, g.reshape(1, d), mods, mods)


def _mm_kernel(a_ref, w_ref, o_ref):
    o_ref[...] = _dot(a_ref[...].astype(BF16), w_ref[...]).astype(o_ref.dtype)


def _matmul(a, w, out_dtype, tm=512, tn=None):
    m, k = a.shape
    n = w.shape[1]
    tn = n if tn is None else tn
    assert m % tm == 0 and n % tn == 0
    return pl.pallas_call(
        _mm_kernel,
        out_shape=jax.ShapeDtypeStruct((m, n), out_dtype),
        grid=(m // tm, n // tn),
        in_specs=[pl.BlockSpec((tm, k), lambda i, j: (i, 0)),
                  pl.BlockSpec((k, tn), lambda i, j: (0, j))],
        out_specs=pl.BlockSpec((tm, tn), lambda i, j: (i, j)),
        compiler_params=_params("parallel", "parallel"),
        name="matmul",
    )(a, w)


def _mm_res_kernel(a_ref, w_ref, res_ref, gate_ref, o_ref):
    y = _dot(a_ref[...].astype(BF16), w_ref[...])
    o_ref[...] = res_ref[...] + gate_ref[0] * y


def _matmul_residual(a, w, res, mods, rows, gate_row):
    m, k = a.shape
    n = w.shape[1]
    tm = rows.tile
    return pl.pallas_call(
        _mm_res_kernel,
        out_shape=jax.ShapeDtypeStruct((m, n), F32),
        grid=(rows.n_tiles,),
        in_specs=[pl.BlockSpec((tm, k), lambda i: (i, 0)),
                  pl.BlockSpec((k, n), lambda i: (0, 0)),
                  pl.BlockSpec((tm, n), lambda i: (i, 0)),
                  pl.BlockSpec((1, 1, n), lambda i: (rows.mod_index(i, gate_row), 0, 0))],
        out_specs=pl.BlockSpec((tm, n), lambda i: (i, 0)),
        compiler_params=_params("parallel"),
        name="matmul_residual",
    )(a, w, res, mods)


def _rope_tables(n_lat, ctx):
    rows_ = n_lat // GRID_W
    row, col = jnp.meshgrid(jnp.arange(rows_), jnp.arange(GRID_W), indexing="ij")
    pos = jnp.stack([row.reshape(-1), col.reshape(-1)], axis=-1).astype(F32)
    n_freq = HEAD_DIM // 4
    inv_freq = ROPE_THETA ** (-jnp.arange(n_freq, dtype=F32) / n_freq)
    ang = jnp.broadcast_to(pos[:, :, None, None] * inv_freq, (n_lat, 2, 2, n_freq)).reshape(n_lat, HEAD_DIM)
    cos = jnp.concatenate([jnp.ones((ctx, HEAD_DIM), F32), jnp.cos(ang)], axis=0)
    sin = jnp.concatenate([jnp.zeros((ctx, HEAD_DIM), F32), jnp.sin(ang)], axis=0)
    return cos, sin


def _rope(x, cos, sin_signed):
    lane = lax.broadcasted_iota(jnp.int32, x.shape, 1)
    first = (lane % 64) < 32
    partner = jnp.where(first, pltpu.roll(x, HEAD_DIM - 32, 1), pltpu.roll(x, 32, 1))
    return x * cos + partner * sin_signed


def _qkv_kernel(u_ref, w_ref, cos_ref, sin_ref, qg_ref, kg_ref, q_ref, kt_ref, v_ref):
    acc = _dot(u_ref[...], w_ref[...])
    cos = cos_ref[...]
    lane = lax.broadcasted_iota(jnp.int32, cos.shape, 1)
    sin_signed = jnp.where((lane % 64) < 32, -sin_ref[...], sin_ref[...])
    q_scale = HEAD_DIM ** -0.5 * math.log2(math.e)
    for h in range(ATTN_HEADS):
        x = _rms(acc[:, h * HEAD_DIM:(h + 1) * HEAD_DIM]) * qg_ref[...]
        q_ref[:, h * HEAD_DIM:(h + 1) * HEAD_DIM] = (_rope(x, cos, sin_signed) * q_scale).astype(BF16)
    k0 = ATTN_HEADS * HEAD_DIM
    v0 = k0 + ATTN_KV_HEADS * HEAD_DIM
    for h in range(ATTN_KV_HEADS):
        x = _rms(acc[:, k0 + h * HEAD_DIM:k0 + (h + 1) * HEAD_DIM]) * kg_ref[...]
        kt_ref[0, h * HEAD_DIM:(h + 1) * HEAD_DIM, :] = _rope(x, cos, sin_signed).T.astype(BF16)
    v_ref[...] = acc[:, v0:].astype(BF16)


def _qkv_project(u, w_qkv, cos, sin, q_gain, k_gain, rows):
    n, d = u.shape
    tm = rows.tile
    nq = ATTN_HEADS * HEAD_DIM
    nkv = ATTN_KV_HEADS * HEAD_DIM
    pb = rows.per_batch
    return pl.pallas_call(
        _qkv_kernel,
        out_shape=(jax.ShapeDtypeStruct((n, nq), BF16),
                   jax.ShapeDtypeStruct((rows.batch, nkv, rows.seq), BF16),
                   jax.ShapeDtypeStruct((n, nkv), BF16)),
        grid=(rows.n_tiles,),
        in_specs=[pl.BlockSpec((tm, d), lambda i: (i, 0)),
                  pl.BlockSpec((d, nq + 2 * nkv), lambda i: (0, 0)),
                  pl.BlockSpec((tm, HEAD_DIM), lambda i: (i % pb, 0)),
                  pl.BlockSpec((tm, HEAD_DIM), lambda i: (i % pb, 0)),
                  pl.BlockSpec((1, HEAD_DIM), lambda i: (0, 0)),
                  pl.BlockSpec((1, HEAD_DIM), lambda i: (0, 0))],
        out_specs=(pl.BlockSpec((tm, nq), lambda i: (i, 0)),
                   pl.BlockSpec((1, nkv, tm), lambda i: (i // pb, 0, i % pb)),
                   pl.BlockSpec((tm, nkv), lambda i: (i, 0))),
        compiler_params=_params("parallel"),
        name="qkv_project",
    )(u, w_qkv, cos, sin, q_gain.reshape(1, HEAD_DIM), k_gain.reshape(1, HEAD_DIM))


def _flash_kernel(q_ref, kt_ref, v_ref, o_ref, m_sc, l_sc, acc_sc, *, tk):
    n_keys = v_ref.shape[0]
    n_tiles = tk // LANES
    m_sc[...] = jnp.full(m_sc.shape, -jnp.inf, F32)
    l_sc[...] = jnp.zeros(l_sc.shape, F32)
    acc_sc[...] = jnp.zeros(acc_sc.shape, F32)

    def body(c, carry):
        off = pl.multiple_of(c * tk, tk)
        kt = kt_ref[0, :, pl.ds(off, tk)]
        vv = v_ref[pl.ds(off, tk), :]
        scores = [_dot(q_ref[:, g * HEAD_DIM:(g + 1) * HEAD_DIM], kt) for g in range(ATTN_GROUP)]
        for g in range(ATTN_GROUP):
            s = scores[g]
            tiles = [s[:, j * LANES:(j + 1) * LANES] for j in range(n_tiles)]
            mx = tiles[0]
            for t in tiles[1:]:
                mx = jnp.maximum(mx, t)
            m_prev = m_sc[g]
            m_new = jnp.maximum(m_prev, jnp.max(mx, axis=-1, keepdims=True))
            alpha = jnp.exp2(m_prev - m_new)
            ps = [jnp.exp2(t - m_new) for t in tiles]
            rs = ps[0]
            for p in ps[1:]:
                rs = rs + p
            l_sc[g] = alpha * l_sc[g] + rs
            p_all = jnp.concatenate([p.astype(BF16) for p in ps], axis=1)
            acc_sc[g] = alpha * acc_sc[g] + _dot(p_all, vv)
            m_sc[g] = m_new
        return carry

    lax.fori_loop(0, n_keys // tk, body, 0)
    for g in range(ATTN_GROUP):
        out = acc_sc[g] / jnp.sum(l_sc[g], axis=-1, keepdims=True)
        o_ref[:, g * HEAD_DIM:(g + 1) * HEAD_DIM] = out.astype(o_ref.dtype)


def _flash(q, kt, v, rows, n_keys, q_tiles, o_prev=None):
    n = q.shape[0]
    tq = rows.tile
    pb = rows.per_batch
    width = ATTN_GROUP * HEAD_DIM
    tk = n_keys
    for cand in (1408, 768, 512, 384, 256, 128):
        if n_keys % cand == 0:
            tk = cand
            break
    key_blocks = rows.seq // n_keys
    in_specs = [pl.BlockSpec((tq, width), lambda b, h, i: (b * pb + i, h)),
                pl.BlockSpec((1, HEAD_DIM, n_keys), lambda b, h, i: (b, h, 0)),
                pl.BlockSpec((n_keys, HEAD_DIM), lambda b, h, i: (b * key_blocks, h))]
    args = [q, kt, v]
    aliases = {}
    if o_prev is not None:
        in_specs.append(pl.BlockSpec(memory_space=pl.ANY))
        args.append(o_prev)
        aliases = {3: 0}

    def kernel(q_ref, kt_ref, v_ref, *rest):
        _flash_kernel(q_ref, kt_ref, v_ref, *rest[-4:], tk=tk)

    return pl.pallas_call(
        kernel,
        out_shape=jax.ShapeDtypeStruct((n, ATTN_HEADS * HEAD_DIM), BF16),
        grid=(rows.batch, ATTN_KV_HEADS, q_tiles),
        in_specs=in_specs,
        out_specs=pl.BlockSpec((tq, width), lambda b, h, i: (b * pb + i, h)),
        scratch_shapes=[pltpu.VMEM((ATTN_GROUP, tq, LANES), F32),
                        pltpu.VMEM((ATTN_GROUP, tq, LANES), F32),
                        pltpu.VMEM((ATTN_GROUP, tq, HEAD_DIM), F32)],
        input_output_aliases=aliases,
        compiler_params=_params("parallel", "parallel", "arbitrary"),
        name="flash_attention",
    )(*args)


def _attention_layer(h, mods, rows, cos, sin, norm_g, w_qkv, q_gain, k_gain, w_o, need_ctx):
    u = _norm_mod(h, norm_g, mods, rows, 0, BF16)
    q, kt, v = _qkv_project(u, w_qkv.astype(BF16), cos, sin, q_gain, k_gain, rows)
    o = _flash(q, kt, v, rows, rows.seq, rows.per_batch)
    if need_ctx:
        o = _flash(q, kt, v, rows, rows.ctx, rows.ctx_tiles, o_prev=o)
    return _matmul_residual(o, w_o.astype(BF16), h, mods, rows, 2)


def _router_kernel(h_ref, g_ref, shift_ref, scale_ref, w_ref, b_ref, t_ref, route_ref):
    t = _rms(h_ref[...]) * g_ref[...]
    t = t * (1.0 + scale_ref[0]) + shift_ref[0]
    t_ref[...] = t
    t_hi, t_lo = _split_bf16(t)
    w_hi, w_lo = _split_bf16(w_ref[...])
    logits = _dot(t_hi, w_hi) + _dot(t_lo, w_hi) + _dot(t_hi, w_lo) + b_ref[...]
    lane = lax.broadcasted_iota(jnp.int32, logits.shape, 1)
    neg = -jnp.inf
    is_group = (lane >= MOE_EXPERTS) & (lane < MOE_EXPERTS + MOE_GROUPS)
    gl = jnp.where(is_group, logits, neg)
    g_max = jnp.max(gl, axis=-1, keepdims=True)
    g_idx = jnp.min(jnp.where(gl == g_max, lane, LANES), axis=-1, keepdims=True) - MOE_EXPERTS
    g_w = 1.0 / jnp.sum(jnp.exp(gl - g_max), axis=-1, keepdims=True)
    in_group = (lane >= g_idx * MOE_PER_GROUP) & (lane < (g_idx + 1) * MOE_PER_GROUP)
    el = jnp.where(in_group, logits, neg)
    m1 = jnp.max(el, axis=-1, keepdims=True)
    i1 = jnp.min(jnp.where(el == m1, lane, LANES), axis=-1, keepdims=True)
    el2 = jnp.where(lane == i1, neg, el)
    m2 = jnp.max(el2, axis=-1, keepdims=True)
    i2 = jnp.min(jnp.where(el2 == m2, lane, LANES), axis=-1, keepdims=True)
    e2 = jnp.exp(m2 - m1)
    w1 = 1.0 / (1.0 + e2)
    w2 = e2 * w1
    route = jnp.where(lane == 0, i1.astype(F32), 0.0) + jnp.where(lane == 1, i2.astype(F32), 0.0)
    route_ref[...] = route + jnp.where(lane == 2, g_w * w1, 0.0) + jnp.where(lane == 3, g_w * w2, 0.0)


def _router(h, norm_g, mods, rows, w_rt, b_rt):
    n, d = h.shape
    tm = rows.tile
    return pl.pallas_call(
        _router_kernel,
        out_shape=(jax.ShapeDtypeStruct((n, d), F32), jax.ShapeDtypeStruct((n, LANES), F32)),
        grid=(rows.n_tiles,),
        in_specs=[pl.BlockSpec((tm, d), lambda i: (i, 0)),
                  pl.BlockSpec((1, d), lambda i: (0, 0)),
                  pl.BlockSpec((1, 1, d), lambda i: (rows.mod_index(i, 3), 0, 0)),
                  pl.BlockSpec((1, 1, d), lambda i: (rows.mod_index(i, 4), 0, 0)),
                  pl.BlockSpec((d, LANES), lambda i: (0, 0)),
                  pl.BlockSpec((1, LANES), lambda i: (0, 0))],
        out_specs=(pl.BlockSpec((tm, d), lambda i: (i, 0)),
                   pl.BlockSpec((tm, LANES), lambda i: (i, 0))),
        compiler_params=_params("parallel"),
        name="moe_router",
    )(h, norm_g.reshape(1, d), mods, mods, w_rt, b_rt)


def _rank_kernel(route_ref, idx_ref, cnt_ref, carry_sc):
    @pl.when(pl.program_id(0) == 0)
    def _():
        carry_sc[...] = jnp.zeros(carry_sc.shape, F32)

    route = route_ref[...]
    tm = route.shape[0]
    lane = lax.broadcasted_iota(jnp.int32, route.shape, 1)
    lane_f = lane.astype(F32)
    e1, e2 = route[:, 0:1], route[:, 1:2]
    hit1, hit2 = lane_f == e1, lane_f == e2
    onehot = jnp.where(hit1 | hit2, 1.0, 0.0)
    row = lax.broadcasted_iota(jnp.int32, (tm, tm), 0)
    col = lax.broadcasted_iota(jnp.int32, (tm, tm), 1)
    earlier = jnp.where(col < row, 1.0, 0.0).astype(BF16)
    before = _dot(earlier, onehot.astype(BF16)) + carry_sc[...]
    r1 = jnp.sum(jnp.where(hit1, before, 0.0), axis=-1, keepdims=True)
    r2 = jnp.sum(jnp.where(hit2, before, 0.0), axis=-1, keepdims=True)
    carry_sc[...] += jnp.sum(onehot, axis=0, keepdims=True)
    packed = (jnp.where(lane == 0, e1, 0.0) + jnp.where(lane == 1, e2, 0.0)
              + jnp.where(lane == 2, r1, 0.0) + jnp.where(lane == 3, r2, 0.0))
    idx_ref[0] = packed.T[0:8, :].astype(jnp.int32)
    cnt_ref[...] = carry_sc[...]


def _rank(route, tm):
    n = route.shape[0]
    return pl.pallas_call(
        _rank_kernel,
        out_shape=(jax.ShapeDtypeStruct((n // tm, 8, tm), jnp.int32), jax.ShapeDtypeStruct((1, LANES), F32)),
        grid=(n // tm,),
        in_specs=[pl.BlockSpec((tm, LANES), lambda i: (i, 0))],
        out_specs=(pl.BlockSpec((1, 8, tm), lambda i: (i, 0, 0)),
                   pl.BlockSpec((1, LANES), lambda i: (0, 0))),
        scratch_shapes=[pltpu.VMEM((1, LANES), F32)],
        compiler_params=_params("arbitrary"),
        name="moe_rank",
    )(route)


ROW_DMA_UNROLL = 8


def _dispatch_kernel(dst_ref, t_ref, xs_in_ref, xs_ref, sem):
    del xs_in_ref
    tm = t_ref.shape[0]

    def row_copy(r, dst):
        return pltpu.make_async_copy(t_ref.at[pl.ds(r, 1), :], xs_ref.at[pl.ds(dst, 1), :], sem)

    def start(i, c):
        for k in range(ROW_DMA_UNROLL):
            r = pl.multiple_of(i * ROW_DMA_UNROLL, ROW_DMA_UNROLL) + k
            row_copy(r, dst_ref[0, 0, r]).start(priority=0)
            row_copy(r, dst_ref[0, 1, r]).start(priority=1)
        return c

    def wait(i, c):
        for _ in range(2 * ROW_DMA_UNROLL):
            row_copy(0, 0).wait()
        return c

    lax.fori_loop(0, tm // ROW_DMA_UNROLL, start, 0)
    lax.fori_loop(0, tm // ROW_DMA_UNROLL, wait, 0)


def _dispatch(t, dst, xs_prev, tm):
    n, d = t.shape
    return pl.pallas_call(
        _dispatch_kernel,
        out_shape=jax.ShapeDtypeStruct(xs_prev.shape, F32),
        grid=(n // tm,),
        in_specs=[pl.BlockSpec((1, 2, tm), lambda i: (i, 0, 0), memory_space=pltpu.SMEM),
                  pl.BlockSpec((tm, d), lambda i: (i, 0)),
                  pl.BlockSpec(memory_space=pl.ANY)],
        out_specs=pl.BlockSpec(memory_space=pl.ANY),
        scratch_shapes=[pltpu.SemaphoreType.DMA(())],
        input_output_aliases={2: 0},
        compiler_params=_params("arbitrary"),
        name="moe_dispatch",
    )(dst, t, xs_prev)


def _grouped_kernel(tile_e_ref, used_ref, x_ref, wg_ref, wu_ref, wd_ref, y_ref):
    del tile_e_ref

    @pl.when(pl.program_id(0) < used_ref[0])
    def _():
        x = x_ref[...].astype(BF16)
        hid = _silu(_dot(x, wg_ref[0])) * _dot(x, wu_ref[0])
        y_ref[...] = _dot(hid.astype(BF16), wd_ref[0])

    @pl.when(pl.program_id(0) >= used_ref[0])
    def _():
        y_ref[...] = jnp.zeros(y_ref.shape, F32)


def _grouped_experts(xs, tile_e, n_used, w_gate, w_up, w_down, tm):
    n_sorted, d = xs.shape
    hid = w_gate.shape[2]
    row = lambda j, te, nu: (jnp.minimum(j, nu[0] - 1), 0)
    wsel = lambda j, te, nu: (te[j], 0, 0)
    return pl.pallas_call(
        _grouped_kernel,
        out_shape=jax.ShapeDtypeStruct((n_sorted, d), F32),
        grid_spec=pltpu.PrefetchScalarGridSpec(
            num_scalar_prefetch=2,
            grid=(n_sorted // tm,),
            in_specs=[pl.BlockSpec((tm, d), row),
                      pl.BlockSpec((1, d, hid), wsel),
                      pl.BlockSpec((1, d, hid), wsel),
                      pl.BlockSpec((1, hid, d), wsel)],
            out_specs=pl.BlockSpec((tm, d), lambda j, te, nu: (j, 0))),
        compiler_params=_params("arbitrary"),
        name="moe_experts",
    )(tile_e, n_used, xs, w_gate, w_up, w_down)


def _combine_kernel(dst_ref, route_ref, res_ref, gate_ref, ys_ref, o_ref, ya_sc, yb_sc, sem):
    tm = res_ref.shape[0]

    def row_copy(src, dst_sc, r):
        return pltpu.make_async_copy(ys_ref.at[pl.ds(src, 1), :], dst_sc.at[pl.ds(r, 1), :], sem)

    def start(i, c):
        for k in range(ROW_DMA_UNROLL):
            r = pl.multiple_of(i * ROW_DMA_UNROLL, ROW_DMA_UNROLL) + k
            row_copy(dst_ref[0, 0, r], ya_sc, r).start(priority=0)
            row_copy(dst_ref[0, 1, r], yb_sc, r).start(priority=1)
        return c

    def wait(i, c):
        for _ in range(2 * ROW_DMA_UNROLL):
            row_copy(0, ya_sc, 0).wait()
        return c

    lax.fori_loop(0, tm // ROW_DMA_UNROLL, start, 0)
    lax.fori_loop(0, tm // ROW_DMA_UNROLL, wait, 0)
    route = route_ref[...]
    mix = route[:, 2:3] * ya_sc[...] + route[:, 3:4] * yb_sc[...]
    o_ref[...] = res_ref[...] + gate_ref[0] * mix


def _combine(ys, dst, route, res, mods, rows):
    n, d = res.shape
    tm = rows.tile
    return pl.pallas_call(
        _combine_kernel,
        out_shape=jax.ShapeDtypeStruct((n, d), F32),
        grid=(rows.n_tiles,),
        in_specs=[pl.BlockSpec((1, 2, tm), lambda i: (i, 0, 0), memory_space=pltpu.SMEM),
                  pl.BlockSpec((tm, LANES), lambda i: (i, 0)),
                  pl.BlockSpec((tm, d), lambda i: (i, 0)),
                  pl.BlockSpec((1, 1, d), lambda i: (rows.mod_index(i, 5), 0, 0)),
                  pl.BlockSpec(memory_space=pl.ANY)],
        out_specs=pl.BlockSpec((tm, d), lambda i: (i, 0)),
        scratch_shapes=[pltpu.VMEM((tm, d), F32), pltpu.VMEM((tm, d), F32),
                        pltpu.SemaphoreType.DMA(())],
        compiler_params=_params("arbitrary"),
        name="moe_combine",
    )(dst, route, res, mods, ys)


def _moe_layer(h, mods, rows, xs_prev, norm_g, w_group, b_group, w_router, b_router, w_gate, w_up, w_down):
    n, d = h.shape
    pad = LANES - MOE_EXPERTS - MOE_GROUPS
    w_rt = jnp.concatenate([w_router, w_group, jnp.zeros((d, pad), F32)], axis=1)
    b_rt = jnp.concatenate([b_router, b_group, jnp.zeros((pad,), F32)]).reshape(1, LANES)
    t, route = _router(h, norm_g, mods, rows, w_rt, b_rt)
    idx, counts = _rank(route, rows.tile)

    tile = MOE_TILE
    max_tiles = xs_prev.shape[0] // tile
    cnt = counts[0, :MOE_EXPERTS].astype(jnp.int32)
    padded = (cnt + tile - 1) // tile * tile
    ends = jnp.cumsum(padded)
    offs = ends - padded
    n_used = (ends[-1] // tile).reshape(1)
    tile_start = jnp.minimum(jnp.arange(max_tiles, dtype=jnp.int32), n_used - 1) * tile
    tile_e = jnp.sum((ends[None, :] <= tile_start[:, None]).astype(jnp.int32), axis=1)
    tile_e = jnp.minimum(tile_e, MOE_EXPERTS - 1)
    dst = idx[:, 2:4, :] + offs[idx[:, 0:2, :]]

    xs = _dispatch(t, dst, xs_prev, rows.tile)
    ys = _grouped_experts(xs, tile_e, n_used, w_gate.astype(BF16), w_up.astype(BF16), w_down.astype(BF16), tile)
    return _combine(ys, dst, route, h, mods, rows), xs


def _scan_block(step, n_ctx_blocks, n_blocks, rev):
    if not rev:
        return step
    return jnp.where(step < n_ctx_blocks, n_ctx_blocks - 1 - step, 2 * n_ctx_blocks + n_blocks - 1 - step)


def _s5_disc_kernel(a_re_ref, a_im_ref, ldt_ref, b_re_ref, b_im_ref, abr_ref, abi_ref, bbr_ref, bbi_ref):
    a_re, a_im = a_re_ref[...], a_im_ref[...]
    dt = jnp.exp(ldt_ref[...])
    mag = jnp.exp(a_re * dt)
    ab_re = mag * jnp.cos(a_im * dt)
    ab_im = mag * jnp.sin(a_im * dt)
    den = a_re * a_re + a_im * a_im
    num_re = ab_re - 1.0
    coef_re = (num_re * a_re + ab_im * a_im) / den
    coef_im = (ab_im * a_re - num_re * a_im) / den
    abr_ref[...] = ab_re
    abi_ref[...] = ab_im
    for c in range(S5_GROUP):
        b_re, b_im = b_re_ref[c], b_im_ref[c]
        bbr_ref[c] = coef_re * b_re - coef_im * b_im
        bbi_ref[c] = coef_re * b_im + coef_im * b_re


def _s5_discretise(a_re, a_im, log_dt, b_re, b_im):
    nd, g, p = a_re.shape
    n = nd * g
    bt = lambda b: jnp.transpose(b, (3, 0, 1, 2)).reshape(S5_GROUP, n, p)
    shp = jax.ShapeDtypeStruct((n, p), F32)
    shp_b = jax.ShapeDtypeStruct((S5_GROUP, n, p), F32)
    return pl.pallas_call(_s5_disc_kernel, out_shape=(shp, shp, shp_b, shp_b), name="s5_discretise")(
        a_re.reshape(n, p), a_im.reshape(n, p), log_dt.reshape(n, 1), bt(b_re), bt(b_im))


def _s5_scan_kernel(u_ref, bm_ref, cm_ref, ar_ref, ai_ref, y_ref, bu_sc, st_sc, *, nb, steps, rev):
    @pl.when(pl.program_id(0) == 0)
    def _():
        st_sc[...] = jnp.zeros(st_sc.shape, F32)

    n_slabs = u_ref.shape[1] // LANES
    ns = S5_SLAB_STATE
    for s in range(n_slabs):
        us = u_ref[:, s * LANES:(s + 1) * LANES].astype(BF16)
        bu_sc[...] = _dot(us, bm_ref[s])
        ar = jnp.broadcast_to(ar_ref[s], (nb, ns))
        ai = jnp.broadcast_to(ai_ref[s], (nb, ns))

        def step(k, carry):
            xr, xi = carry
            t = (steps - 1 - k) if rev else k
            r0 = pl.multiple_of(t * nb, nb)
            br = bu_sc[pl.ds(r0, nb), 0:ns]
            bi = bu_sc[pl.ds(r0, nb), ns:2 * ns]
            nr = ar * xr - ai * xi + br
            ni = ar * xi + ai * xr + bi
            bu_sc[pl.ds(r0, nb), 0:ns] = nr
            bu_sc[pl.ds(r0, nb), ns:2 * ns] = ni
            return nr, ni

        xr, xi = lax.fori_loop(0, steps, step, (st_sc[s, 0], st_sc[s, 1]), unroll=8)
        st_sc[s, 0] = xr
        st_sc[s, 1] = xi
        y_ref[:, s * LANES:(s + 1) * LANES] = _dot(bu_sc[...].astype(BF16), cm_ref[s])


def _s5_scan(u_tm, bmat, cmat, ab_re, ab_im, nb, n_ctx, rev):
    n, d = u_tm.shape
    steps = S5_STEPS
    blk = steps * nb
    n_blocks = n // blk
    n_ctx_blocks = n_ctx // steps
    n_slabs = d // LANES
    ns = S5_SLAB_STATE
    idx = lambda s: (_scan_block(s, n_ctx_blocks, n_blocks - n_ctx_blocks, rev), 0)
    return pl.pallas_call(
        functools.partial(_s5_scan_kernel, nb=nb, steps=steps, rev=rev),
        out_shape=jax.ShapeDtypeStruct((n, d), F32),
        grid=(n_blocks,),
        in_specs=[pl.BlockSpec((blk, d), idx),
                  pl.BlockSpec((n_slabs, LANES, 2 * ns), lambda s: (0, 0, 0)),
                  pl.BlockSpec((n_slabs, 2 * ns, LANES), lambda s: (0, 0, 0)),
                  pl.BlockSpec((n_slabs, 1, ns), lambda s: (0, 0, 0)),
                  pl.BlockSpec((n_slabs, 1, ns), lambda s: (0, 0, 0))],
        out_specs=pl.BlockSpec((blk, d), idx),
        scratch_shapes=[pltpu.VMEM((blk, 2 * ns), F32),
                        pltpu.VMEM((n_slabs, 2, nb, ns), F32)],
        compiler_params=_params("arbitrary"),
        name="s5_scan_bwd" if rev else "s5_scan_fwd",
    )(u_tm, bmat, cmat, ab_re, ab_im)


def _gelu_tanh(x):
    return 0.5 * x * (1.0 + jnp.tanh(math.sqrt(2.0 / math.pi) * (x + 0.044715 * (x * x * x))))


def _s5_glu_kernel(yf_ref, yb_ref, u_ref, d_ref, wa_ref, wb_ref, ba_ref, bb_ref, o_ref):
    y = yf_ref[...] + yb_ref[...] + u_ref[...] * d_ref[...]
    z = _gelu_tanh(y).astype(BF16)
    a = _dot(z, wa_ref[...]) + ba_ref[...]
    b = _dot(z, wb_ref[...]) + bb_ref[...]
    o_ref[...] = a * _sigmoid(b)


def _s5_glu(y_f, y_b, u, d_skip, w_glu, b_glu, tm):
    n, d = u.shape
    row = pl.BlockSpec((tm, d), lambda i: (i, 0))
    vec = pl.BlockSpec((1, d), lambda i: (0, 0))
    return pl.pallas_call(
        _s5_glu_kernel,
        out_shape=jax.ShapeDtypeStruct((n, d), F32),
        grid=(n // tm,),
        in_specs=[row, row, row, vec,
                  pl.BlockSpec((d, d), lambda i: (0, 0)),
                  pl.BlockSpec((d, d), lambda i: (0, 1)),
                  vec,
                  pl.BlockSpec((1, d), lambda i: (0, 1))],
        out_specs=row,
        compiler_params=_params("parallel"),
        name="s5_glu",
    )(y_f, y_b, u, d_skip.reshape(1, d), w_glu, w_glu, b_glu.reshape(1, 2 * d), b_glu.reshape(1, 2 * d))


def _residual_kernel(h_ref, y_ref, gate_ref, o_ref):
    o_ref[...] = h_ref[...] + gate_ref[0] * y_ref[...]


def _residual(h, y, mods, rows, gate_row):
    n, d = h.shape
    tm = rows.tile
    row = pl.BlockSpec((tm, d), lambda i: (i, 0))
    return pl.pallas_call(
        _residual_kernel,
        out_shape=jax.ShapeDtypeStruct((n, d), F32),
        grid=(rows.n_tiles,),
        in_specs=[row, row, pl.BlockSpec((1, 1, d), lambda i: (rows.mod_index(i, gate_row), 0, 0))],
        out_specs=row,
        compiler_params=_params("parallel"),
        name="residual",
    )(h, y, mods)


def _s5_layer(h, mods, rows, norm_g, a_re, a_im, log_dt, b_re, b_im, c_re, c_im, d_skip, w_glu, b_glu):
    nb, seq = rows.batch, rows.seq
    d = h.shape[1]
    n_slabs = d // LANES
    sg = S5_SLAB_GROUPS
    ab_re, ab_im, bb_re, bb_im = _s5_discretise(a_re, a_im, log_dt, b_re, b_im)
    eye = jnp.eye(sg, dtype=F32)

    def b_blocks(bb):
        t = bb.reshape(S5_GROUP, 2, n_slabs, sg, S5_STATE)
        t = jnp.einsum("cdsgp,gh->dsgchp", t, eye)
        return t.reshape(2, n_slabs, LANES, S5_SLAB_STATE)

    def c_blocks(cc):
        t = cc.reshape(2, n_slabs, sg, S5_GROUP, S5_STATE)
        t = jnp.einsum("dsgcp,gh->dshpgc", t, eye)
        return t.reshape(2, n_slabs, S5_SLAB_STATE, LANES)

    bmat = jnp.concatenate([b_blocks(bb_re), b_blocks(bb_im)], axis=-1).astype(BF16)
    cmat = jnp.concatenate([c_blocks(c_re), -c_blocks(c_im)], axis=2).astype(BF16)
    ab_re = ab_re.reshape(2, n_slabs, 1, S5_SLAB_STATE)
    ab_im = ab_im.reshape(2, n_slabs, 1, S5_SLAB_STATE)

    u = _norm_mod(h, norm_g, mods, rows, 0, F32)
    u_tm = jnp.transpose(u.reshape(nb, seq, d), (1, 0, 2)).reshape(seq * nb, d)
    y_f = _s5_scan(u_tm, bmat[0], cmat[0], ab_re[0], ab_im[0], nb, rows.ctx, False)
    y_b = _s5_scan(u_tm, bmat[1], cmat[1], ab_re[1], ab_im[1], nb, rows.ctx, True)
    mix_tm = _s5_glu(y_f, y_b, u_tm, d_skip, w_glu.astype(BF16), b_glu, S5_STEPS * nb)
    mix = jnp.transpose(mix_tm.reshape(seq, nb, d), (1, 0, 2)).reshape(nb * seq, d)
    return _residual(h, mix, mods, rows, 2)


def _ssd_conv_kernel(x_ref, w_ref, b_ref, o_ref, pad_sc, *, ctx, n_lat, chunk):
    tc = x_ref.shape[2]
    zeros = jnp.zeros((8, tc), F32)
    lat0 = 16 + ctx
    pad_sc[0:8, :] = zeros
    pad_sc[8 + ctx:lat0, :] = zeros
    pad_sc[lat0 + n_lat:lat0 + n_lat + 8, :] = zeros
    half = SSD_CONV // 2
    segments = ((0, ctx, 8), (ctx, n_lat, lat0))
    for src0, length, dst0 in segments:
        for r in range(0, length, chunk):
            pad_sc[dst0 + r:dst0 + r + chunk, :] = x_ref[0, src0 + r:src0 + r + chunk, :]
    w = w_ref[...]
    for src0, length, dst0 in segments:
        for r in range(0, length, chunk):
            acc = jnp.broadcast_to(b_ref[...], (chunk, tc))
            for k in range(SSD_CONV):
                lo = dst0 + r + k - half
                acc = acc + w[k:k + 1, :] * pad_sc[lo:lo + chunk, :]
            o_ref[0, src0 + r:src0 + r + chunk, :] = _silu(acc)


def _ssd_conv(xbc, conv_w, conv_b, ctx):
    nb, seq, ch = xbc.shape
    tc = LANES
    n_lat = seq - ctx
    return pl.pallas_call(
        functools.partial(_ssd_conv_kernel, ctx=ctx, n_lat=n_lat, chunk=ROW_TILE),
        out_shape=jax.ShapeDtypeStruct((nb, seq, ch), F32),
        grid=(nb, ch // tc),
        in_specs=[pl.BlockSpec((1, seq, tc), lambda b, j: (b, 0, j)),
                  pl.BlockSpec((SSD_CONV, tc), lambda b, j: (0, j)),
                  pl.BlockSpec((1, tc), lambda b, j: (0, j))],
        out_specs=pl.BlockSpec((1, seq, tc), lambda b, j: (b, 0, j)),
        scratch_shapes=[pltpu.VMEM((seq + 24, tc), F32)],
        compiler_params=_params("parallel", "parallel"),
        name="ssd_conv",
    )(xbc, conv_w, conv_b.reshape(1, ch))


def _softplus(x):
    return jnp.maximum(x, 0.0) + jnp.log(1.0 + jnp.exp(-jnp.abs(x)))


def _ssd_scan_kernel(xbc_ref, dt_ref, dtb_ref, a_ref, y_ref, st_sc, *, rev, col0, n_heads):
    @pl.when(pl.program_id(1) == 0)
    def _():
        st_sc[...] = jnp.zeros(st_sc.shape, F32)

    q = xbc_ref.shape[1]
    hpg = n_heads // SSD_GROUPS
    inner = n_heads * SSD_HEAD_DIM
    dt = _softplus(dt_ref[0] + dtb_ref[...])
    a = dt * a_ref[...]
    row = lax.broadcasted_iota(jnp.int32, (q, q), 0)
    col = lax.broadcasted_iota(jnp.int32, (q, q), 1)
    tri = (col >= row) if rev else (col <= row)
    tri_b = jnp.where(tri, 1.0, 0.0).astype(BF16)
    a_hi, a_lo = _split_bf16(a)
    cum = _dot(tri_b, a_hi) + _dot(tri_b, a_lo)
    cum_t = cum.T
    total = cum[0:1, :] if rev else cum[q - 1:q, :]
    for g in range(SSD_GROUPS):
        bg = xbc_ref[0, :, inner + g * SSD_STATE:inner + (g + 1) * SSD_STATE]
        cg = xbc_ref[0, :, inner + (SSD_GROUPS + g) * SSD_STATE:inner + (SSD_GROUPS + g + 1) * SSD_STATE]
        cg_b = cg.astype(BF16)
        cb = lax.dot_general(cg_b, bg.astype(BF16), (((1,), (1,)), ((), ())), preferred_element_type=F32)
        bg_t = bg.T
        for hh in range(hpg):
            hd = g * hpg + hh
            c = col0 + hd
            cum_c = cum[:, c:c + 1]
            cum_r = cum_t[c:c + 1, :]
            decay = jnp.exp(jnp.where(tri, cum_c - cum_r, -jnp.inf))
            xdt = (xbc_ref[0, :, hd * SSD_HEAD_DIM:(hd + 1) * SSD_HEAD_DIM] * dt[:, c:c + 1]).astype(BF16)
            state = st_sc[hd]
            y = _dot((cb * decay).astype(BF16), xdt) + jnp.exp(cum_c) * _dot(cg_b, state.astype(BF16))
            y_ref[0, :, hd * SSD_HEAD_DIM:(hd + 1) * SSD_HEAD_DIM] = y
            tot = total[:, c:c + 1]
            tail = jnp.exp(tot - cum_r)
            st_sc[hd] = jnp.exp(tot) * state + _dot((bg_t * tail).astype(BF16), xdt)


def _ssd_scan(xbc, dt_raw, dt_bias, a_neg, ctx, rev, n_heads):
    nb, seq, ch = xbc.shape
    q = SSD_CHUNK
    n_blocks = seq // q
    n_ctx_blocks = ctx // q
    inner = n_heads * SSD_HEAD_DIM
    blk = lambda b, s: (b, _scan_block(s, n_ctx_blocks, n_blocks - n_ctx_blocks, rev), 0)
    return pl.pallas_call(
        functools.partial(_ssd_scan_kernel, rev=rev, col0=n_heads if rev else 0, n_heads=n_heads),
        out_shape=jax.ShapeDtypeStruct((nb, seq, inner), F32),
        grid=(nb, n_blocks),
        in_specs=[pl.BlockSpec((1, q, ch), blk),
                  pl.BlockSpec((1, q, LANES), blk),
                  pl.BlockSpec((1, LANES), lambda b, s: (0, 0)),
                  pl.BlockSpec((1, LANES), lambda b, s: (0, 0))],
        out_specs=pl.BlockSpec((1, q, inner), blk),
        scratch_shapes=[pltpu.VMEM((n_heads, SSD_STATE, SSD_HEAD_DIM), F32)],
        compiler_params=_params("parallel", "arbitrary"),
        name="ssd_scan_bwd" if rev else "ssd_scan_fwd",
    )(xbc, dt_raw, dt_bias, a_neg)


def _ssd_finish_kernel(yf_ref, yb_ref, x_ref, z_ref, d_ref, g_ref, o_ref):
    y = yf_ref[...] + yb_ref[...] + x_ref[...] * d_ref[...]
    o_ref[...] = (_rms(y * _silu(z_ref[...])) * g_ref[...]).astype(o_ref.dtype)


def _ssd_finish(y_f, y_b, xbc, z, d_skip, norm_g, tm):
    n, inner = z.shape
    row = pl.BlockSpec((tm, inner), lambda i: (i, 0))
    vec = pl.BlockSpec((1, inner), lambda i: (0, 0))
    return pl.pallas_call(
        _ssd_finish_kernel,
        out_shape=jax.ShapeDtypeStruct((n, inner), BF16),
        grid=(n // tm,),
        in_specs=[row, row, row, row, vec, vec],
        out_specs=row,
        compiler_params=_params("parallel"),
        name="ssd_finish",
    )(y_f, y_b, xbc, z, d_skip, norm_g)


def _ssd_layer(h, mods, rows, norm_g, w_in, conv_w, conv_b, dt_bias, a_log, d_skip, ssd_norm_g, w_out):
    nb, seq = rows.batch, rows.seq
    n = nb * seq
    n_heads = dt_bias.shape[1]
    inner = n_heads * SSD_HEAD_DIM
    conv_ch = inner + 2 * SSD_GROUPS * SSD_STATE
    u = _norm_mod(h, norm_g, mods, rows, 0, BF16)
    w_in = w_in.astype(BF16)
    d = w_in.shape[0]
    z = _matmul(u, w_in[:, :inner], F32, tn=1024)
    xbc = _matmul(u, w_in[:, inner:inner + conv_ch], F32, tn=1024)
    w_dt = jnp.concatenate([w_in[:, inner + conv_ch:], jnp.zeros((d, LANES - 2 * n_heads), BF16)], axis=1)
    dt_raw = _matmul(u, w_dt, F32)
    xbc = _ssd_conv(xbc.reshape(nb, seq, conv_ch), conv_w, conv_b, rows.ctx)
    pad = jnp.zeros((LANES - 2 * n_heads,), F32)
    dtb = jnp.concatenate([dt_bias.reshape(-1), pad]).reshape(1, LANES)
    a_neg = jnp.concatenate([a_log.reshape(-1), pad]).reshape(1, LANES)
    a_neg = _neg_exp(a_neg, 2 * n_heads)
    dt3 = dt_raw.reshape(nb, seq, LANES)
    y_f = _ssd_scan(xbc, dt3, dtb, a_neg, rows.ctx, False, n_heads)
    y_b = _ssd_scan(xbc, dt3, dtb, a_neg, rows.ctx, True, n_heads)
    dsk = jnp.repeat(d_skip, SSD_HEAD_DIM).reshape(1, inner)
    g = _ssd_finish(y_f.reshape(n, inner), y_b.reshape(n, inner), xbc.reshape(n, conv_ch), z, dsk,
                    ssd_norm_g.reshape(1, inner), rows.tile)
    return _matmul_residual(g, w_out.astype(BF16), h, mods, rows, 2)


def _neg_exp_kernel(x_ref, o_ref, *, valid):
    lane = lax.broadcasted_iota(jnp.int32, x_ref.shape, 1)
    o_ref[...] = jnp.where(lane < valid, -jnp.exp(x_ref[...]), 0.0)


def _neg_exp(x, valid):
    return pl.pallas_call(functools.partial(_neg_exp_kernel, valid=valid),
                          out_shape=jax.ShapeDtypeStruct(x.shape, F32), name="ssd_decay_rate")(x)


def kernel(x, c, ctx, c_ctx, mod_w, mod_b, norm1_g, norm2_g, attn_w_qkv, attn_q_gain, attn_k_gain, attn_w_o, s5_a_re, s5_a_im, s5_log_dt, s5_b_re, s5_b_im, s5_c_re, s5_c_im, s5_d, s5_w_glu, s5_b_glu, ssd_w_in, ssd_conv_w, ssd_conv_b, ssd_dt_bias, ssd_a_log, ssd_d, ssd_norm_g, ssd_w_out, moe_w_group, moe_b_group, moe_w_router, moe_b_router, moe_w_gate, moe_w_up, moe_w_down):
    batch, n_lat, d = x.shape
    n_ctx = ctx.shape[1]
    depth = mod_w.shape[0]
    seq = n_ctx + n_lat
    rows = _Rows(batch, seq, n_ctx, ROW_TILE)

    n_cond = -(-(batch + 1) // 8) * 8
    cond = jnp.concatenate([c, c_ctx[None, :], jnp.zeros((n_cond - batch - 1, d), F32)], axis=0)
    mods_all = _modulation(cond, mod_w, mod_b).reshape(depth, n_cond * 6, 1, d)

    cos, sin = _rope_tables(n_lat, n_ctx)
    h = jnp.concatenate([ctx, x], axis=1).reshape(batch * seq, d)

    max_tiles = -(-2 * batch * seq // MOE_TILE) + MOE_EXPERTS
    xs_buf = jnp.zeros((max_tiles * MOE_TILE, d), F32)
    for i in range(depth):
        need_ctx = i < depth - 1
        kind, j = i % N_MIXERS, i // N_MIXERS
        mods = mods_all[i]
        if kind == 0:
            h = _attention_layer(h, mods, rows, cos, sin, norm1_g[i], attn_w_qkv[j], attn_q_gain[j],
                                 attn_k_gain[j], attn_w_o[j], need_ctx)
        elif kind == 1:
            h = _s5_layer(h, mods, rows, norm1_g[i], s5_a_re[j], s5_a_im[j], s5_log_dt[j], s5_b_re[j], s5_b_im[j],
                          s5_c_re[j], s5_c_im[j], s5_d[j], s5_w_glu[j], s5_b_glu[j])
        else:
            h = _ssd_layer(h, mods, rows, norm1_g[i], ssd_w_in[j], ssd_conv_w[j], ssd_conv_b[j], ssd_dt_bias[j],
                           ssd_a_log[j], ssd_d[j], ssd_norm_g[j], ssd_w_out[j])
        h, xs_buf = _moe_layer(h, mods, rows, xs_buf, norm2_g[i], moe_w_group[i], moe_b_group[i], moe_w_router[i],
                               moe_b_router[i], moe_w_gate[i], moe_w_up[i], moe_w_down[i])
    return h.reshape(batch, seq, d)[:, n_ctx:, :]
```

```python
import functools
import math

import jax
import jax.numpy as jnp
from jax import lax
from jax.experimental import pallas as pl
from jax.experimental.pallas import tpu as pltpu

F32 = jnp.float32
BF16 = jnp.bfloat16

NORM_EPS = 1e-6
GRID_W = 64
N_MIXERS = 3

ATTN_HEADS = 8
ATTN_KV_HEADS = 2
ATTN_GROUP = ATTN_HEADS // ATTN_KV_HEADS
HEAD_DIM = 128
ROPE_THETA = 10000.0

S5_GROUP = 16
S5_STATE = 64
S5_SLAB_GROUPS = 8
S5_SLAB_STATE = S5_SLAB_GROUPS * S5_STATE
S5_STEPS = 64

SSD_HEAD_DIM = 64
SSD_GROUPS = 4
SSD_STATE = 128
SSD_CONV = 5
SSD_CHUNK = 128

MOE_GROUPS = 4
MOE_PER_GROUP = 8
MOE_EXPERTS = MOE_GROUPS * MOE_PER_GROUP
MOE_HIDDEN = 256
MOE_TILE = 512

LANES = 128
ROW_TILE = 256
VMEM_LIMIT = 56 * 1024 * 1024


def _params(*sem):
    return pltpu.CompilerParams(dimension_semantics=sem, vmem_limit_bytes=VMEM_LIMIT)


def _rms(x):
    return x * lax.rsqrt(jnp.mean(x * x, axis=-1, keepdims=True) + NORM_EPS)


def _silu(x):
    return x * (1.0 / (1.0 + jnp.exp(-x)))


def _sigmoid(x):
    return 1.0 / (1.0 + jnp.exp(-x))


def _dot(a, b):
    return jnp.dot(a, b, preferred_element_type=F32)


def _split_bf16(x):
    hi = x.astype(BF16)
    lo = (x - hi.astype(F32)).astype(BF16)
    return hi, lo


class _Rows:
    def __init__(self, batch, seq, ctx, tile):
        assert seq % tile == 0 and ctx % tile == 0
        self.batch, self.seq, self.ctx, self.tile = batch, seq, ctx, tile
        self.per_batch = seq // tile
        self.ctx_tiles = ctx // tile
        self.n_tiles = batch * self.per_batch

    def mod_index(self, i, row):
        b = i // self.per_batch
        is_ctx = (i % self.per_batch) < self.ctx_tiles
        return jnp.where(is_ctx, self.batch, b) * 6 + row


def _mod_kernel(c_ref, w_ref, b_ref, o_ref):
    a = _silu(c_ref[...]).astype(BF16)
    o_ref[0] = _dot(a, w_ref[0].astype(BF16)) + b_ref[0]


def _modulation(cond, mod_w, mod_b):
    depth, d, n = mod_w.shape
    rows = cond.shape[0]
    tn = 1536
    return pl.pallas_call(
        _mod_kernel,
        out_shape=jax.ShapeDtypeStruct((depth, rows, n), F32),
        grid=(depth, n // tn),
        in_specs=[pl.BlockSpec((rows, d), lambda l, j: (0, 0)),
                  pl.BlockSpec((1, d, tn), lambda l, j: (l, 0, j)),
                  pl.BlockSpec((1, 1, tn), lambda l, j: (l, 0, j))],
        out_specs=pl.BlockSpec((1, rows, tn), lambda l, j: (l, 0, j)),
        compiler_params=_params("parallel", "parallel"),
        name="modulation",
    )(cond, mod_w, mod_b.reshape(depth, 1, n))


def _norm_mod_kernel(h_ref, g_ref, shift_ref, scale_ref, o_ref):
    u = _rms(h_ref[...]) * g_ref[...]
    o_ref[...] = (u * (1.0 + scale_ref[0]) + shift_ref[0]).astype(o_ref.dtype)


def _norm_mod(h, g, mods, rows, shift_row, out_dtype):
    n, d = h.shape
    tm = rows.tile
    return pl.pallas_call(
        _norm_mod_kernel,
        out_shape=jax.ShapeDtypeStruct((n, d), out_dtype),
        grid=(rows.n_tiles,),
        in_specs=[pl.BlockSpec((tm, d), lambda i: (i, 0)),
                  pl.BlockSpec((1, d), lambda i: (0, 0)),
                  pl.BlockSpec((1, 1, d), lambda i: (rows.mod_index(i, shift_row), 0, 0)),
                  pl.BlockSpec((1, 1, d), lambda i: (rows.mod_index(i, shift_row + 1), 0, 0))],
        out_specs=pl.BlockSpec((tm, d), lambda i: (i, 0)),
        compiler_params=_params("parallel"),
        name="norm_mod",
    )(h, g.reshape(1, d), mods, mods)


def _mm_kernel(a_ref, w_ref, o_ref):
    o_ref[...] = _dot(a_ref[...].astype(BF16), w_ref[...]).astype(o_ref.dtype)


def _matmul(a, w, out_dtype, tm=512, tn=None):
    m, k = a.shape
    n = w.shape[1]
    tn = n if tn is None else tn
    assert m % tm == 0 and n % tn == 0
    return pl.pallas_call(
        _mm_kernel,
        out_shape=jax.ShapeDtypeStruct((m, n), out_dtype),
        grid=(m // tm, n // tn),
        in_specs=[pl.BlockSpec((tm, k), lambda i, j: (i, 0)),
                  pl.BlockSpec((k, tn), lambda i, j: (0, j))],
        out_specs=pl.BlockSpec((tm, tn), lambda i, j: (i, j)),
        compiler_params=_params("parallel", "parallel"),
        name="matmul",
    )(a, w)


def _mm_res_kernel(a_ref, w_ref, res_ref, gate_ref, o_ref):
    y = _dot(a_ref[...].astype(BF16), w_ref[...])
    o_ref[...] = res_ref[...] + gate_ref[0] * y


def _matmul_residual(a, w, res, mods, rows, gate_row):
    m, k = a.shape
    n = w.shape[1]
    tm = rows.tile
    return pl.pallas_call(
        _mm_res_kernel,
        out_shape=jax.ShapeDtypeStruct((m, n), F32),
        grid=(rows.n_tiles,),
        in_specs=[pl.BlockSpec((tm, k), lambda i: (i, 0)),
                  pl.BlockSpec((k, n), lambda i: (0, 0)),
                  pl.BlockSpec((tm, n), lambda i: (i, 0)),
                  pl.BlockSpec((1, 1, n), lambda i: (rows.mod_index(i, gate_row), 0, 0))],
        out_specs=pl.BlockSpec((tm, n), lambda i: (i, 0)),
        compiler_params=_params("parallel"),
        name="matmul_residual",
    )(a, w, res, mods)


def _rope_tables(n_lat, ctx):
    rows_ = n_lat // GRID_W
    row, col = jnp.meshgrid(jnp.arange(rows_), jnp.arange(GRID_W), indexing="ij")
    pos = jnp.stack([row.reshape(-1), col.reshape(-1)], axis=-1).astype(F32)
    n_freq = HEAD_DIM // 4
    inv_freq = ROPE_THETA ** (-jnp.arange(n_freq, dtype=F32) / n_freq)
    ang = jnp.broadcast_to(pos[:, :, None, None] * inv_freq, (n_lat, 2, 2, n_freq)).reshape(n_lat, HEAD_DIM)
    cos = jnp.concatenate([jnp.ones((ctx, HEAD_DIM), F32), jnp.cos(ang)], axis=0)
    sin = jnp.concatenate([jnp.zeros((ctx, HEAD_DIM), F32), jnp.sin(ang)], axis=0)
    return cos, sin


def _rope(x, cos, sin_signed):
    lane = lax.broadcasted_iota(jnp.int32, x.shape, 1)
    first = (lane % 64) < 32
    partner = jnp.where(first, pltpu.roll(x, HEAD_DIM - 32, 1), pltpu.roll(x, 32, 1))
    return x * cos + partner * sin_signed


def _qkv_kernel(u_ref, w_ref, cos_ref, sin_ref, qg_ref, kg_ref, q_ref, kt_ref, v_ref):
    acc = _dot(u_ref[...], w_ref[...])
    cos = cos_ref[...]
    lane = lax.broadcasted_iota(jnp.int32, cos.shape, 1)
    sin_signed = jnp.where((lane % 64) < 32, -sin_ref[...], sin_ref[...])
    q_scale = HEAD_DIM ** -0.5 * math.log2(math.e)
    for h in range(ATTN_HEADS):
        x = _rms(acc[:, h * HEAD_DIM:(h + 1) * HEAD_DIM]) * qg_ref[...]
        q_ref[:, h * HEAD_DIM:(h + 1) * HEAD_DIM] = (_rope(x, cos, sin_signed) * q_scale).astype(BF16)
    k0 = ATTN_HEADS * HEAD_DIM
    v0 = k0 + ATTN_KV_HEADS * HEAD_DIM
    for h in range(ATTN_KV_HEADS):
        x = _rms(acc[:, k0 + h * HEAD_DIM:k0 + (h + 1) * HEAD_DIM]) * kg_ref[...]
        kt_ref[0, h * HEAD_DIM:(h + 1) * HEAD_DIM, :] = _rope(x, cos, sin_signed).T.astype(BF16)
    v_ref[...] = acc[:, v0:].astype(BF16)


def _qkv_project(u, w_qkv, cos, sin, q_gain, k_gain, rows):
    n, d = u.shape
    tm = rows.tile
    nq = ATTN_HEADS * HEAD_DIM
    nkv = ATTN_KV_HEADS * HEAD_DIM
    pb = rows.per_batch
    return pl.pallas_call(
        _qkv_kernel,
        out_shape=(jax.ShapeDtypeStruct((n, nq), BF16),
                   jax.ShapeDtypeStruct((rows.batch, nkv, rows.seq), BF16),
                   jax.ShapeDtypeStruct((n, nkv), BF16)),
        grid=(rows.n_tiles,),
        in_specs=[pl.BlockSpec((tm, d), lambda i: (i, 0)),
                  pl.BlockSpec((d, nq + 2 * nkv), lambda i: (0, 0)),
                  pl.BlockSpec((tm, HEAD_DIM), lambda i: (i % pb, 0)),
                  pl.BlockSpec((tm, HEAD_DIM), lambda i: (i % pb, 0)),
                  pl.BlockSpec((1, HEAD_DIM), lambda i: (0, 0)),
                  pl.BlockSpec((1, HEAD_DIM), lambda i: (0, 0))],
        out_specs=(pl.BlockSpec((tm, nq), lambda i: (i, 0)),
                   pl.BlockSpec((1, nkv, tm), lambda i: (i // pb, 0, i % pb)),
                   pl.BlockSpec((tm, nkv), lambda i: (i, 0))),
        compiler_params=_params("parallel"),
        name="qkv_project",
    )(u, w_qkv, cos, sin, q_gain.reshape(1, HEAD_DIM), k_gain.reshape(1, HEAD_DIM))


def _flash_kernel(q_ref, kt_ref, v_ref, o_ref, m_sc, l_sc, acc_sc, *, tk):
    n_keys = v_ref.shape[0]
    n_tiles = tk // LANES
    m_sc[...] = jnp.full(m_sc.shape, -jnp.inf, F32)
    l_sc[...] = jnp.zeros(l_sc.shape, F32)
    acc_sc[...] = jnp.zeros(acc_sc.shape, F32)

    def body(c, carry):
        off = pl.multiple_of(c * tk, tk)
        kt = kt_ref[0, :, pl.ds(off, tk)]
        vv = v_ref[pl.ds(off, tk), :]
        scores = [_dot(q_ref[:, g * HEAD_DIM:(g + 1) * HEAD_DIM], kt) for g in range(ATTN_GROUP)]
        for g in range(ATTN_GROUP):
            s = scores[g]
            tiles = [s[:, j * LANES:(j + 1) * LANES] for j in range(n_tiles)]
            mx = tiles[0]
            for t in tiles[1:]:
                mx = jnp.maximum(mx, t)
            m_prev = m_sc[g]
            m_new = jnp.maximum(m_prev, jnp.max(mx, axis=-1, keepdims=True))
            alpha = jnp.exp2(m_prev - m_new)
            ps = [jnp.exp2(t - m_new) for t in tiles]
            rs = ps[0]
            for p in ps[1:]:
                rs = rs + p
            l_sc[g] = alpha * l_sc[g] + rs
            p_all = jnp.concatenate([p.astype(BF16) for p in ps], axis=1)
            acc_sc[g] = alpha * acc_sc[g] + _dot(p_all, vv)
            m_sc[g] = m_new
        return carry

    lax.fori_loop(0, n_keys // tk, body, 0)
    for g in range(ATTN_GROUP):
        out = acc_sc[g] / jnp.sum(l_sc[g], axis=-1, keepdims=True)
        o_ref[:, g * HEAD_DIM:(g + 1) * HEAD_DIM] = out.astype(o_ref.dtype)


def _flash(q, kt, v, rows, n_keys, q_tiles, o_prev=None):
    n = q.shape[0]
    tq = rows.tile
    pb = rows.per_batch
    width = ATTN_GROUP * HEAD_DIM
    tk = n_keys
    for cand in (1408, 768, 512, 384, 256, 128):
        if n_keys % cand == 0:
            tk = cand
            break
    key_blocks = rows.seq // n_keys
    in_specs = [pl.BlockSpec((tq, width), lambda b, h, i: (b * pb + i, h)),
                pl.BlockSpec((1, HEAD_DIM, n_keys), lambda b, h, i: (b, h, 0)),
                pl.BlockSpec((n_keys, HEAD_DIM), lambda b, h, i: (b * key_blocks, h))]
    args = [q, kt, v]
    aliases = {}
    if o_prev is not None:
        in_specs.append(pl.BlockSpec(memory_space=pl.ANY))
        args.append(o_prev)
        aliases = {3: 0}

    def kernel(q_ref, kt_ref, v_ref, *rest):
        _flash_kernel(q_ref, kt_ref, v_ref, *rest[-4:], tk=tk)

    return pl.pallas_call(
        kernel,
        out_shape=jax.ShapeDtypeStruct((n, ATTN_HEADS * HEAD_DIM), BF16),
        grid=(rows.batch, ATTN_KV_HEADS, q_tiles),
        in_specs=in_specs,
        out_specs=pl.BlockSpec((tq, width), lambda b, h, i: (b * pb + i, h)),
        scratch_shapes=[pltpu.VMEM((ATTN_GROUP, tq, LANES), F32),
                        pltpu.VMEM((ATTN_GROUP, tq, LANES), F32),
                        pltpu.VMEM((ATTN_GROUP, tq, HEAD_DIM), F32)],
        input_output_aliases=aliases,
        compiler_params=_params("parallel", "parallel", "arbitrary"),
        name="flash_attention",
    )(*args)


def _attention_layer(h, mods, rows, cos, sin, norm_g, w_qkv, q_gain, k_gain, w_o, need_ctx):
    u = _norm_mod(h, norm_g, mods, rows, 0, BF16)
    q, kt, v = _qkv_project(u, w_qkv.astype(BF16), cos, sin, q_gain, k_gain, rows)
    o = _flash(q, kt, v, rows, rows.seq, rows.per_batch)
    if need_ctx:
        o = _flash(q, kt, v, rows, rows.ctx, rows.ctx_tiles, o_prev=o)
    return _matmul_residual(o, w_o.astype(BF16), h, mods, rows, 2)


TOKEN_ROWS = 8


def _store_token_tiles(ref, x):
    n_tok = x.shape[0]
    for c in range(TOKEN_ROWS):
        ref[pl.ds(c, n_tok, stride=TOKEN_ROWS), :] = x[:, c * LANES:(c + 1) * LANES]


def _token_rows(r):
    start = r * TOKEN_ROWS
    return pl.ds(start if isinstance(start, int) else pl.multiple_of(start, TOKEN_ROWS), TOKEN_ROWS)


def _load_token_chunk(ref, c, n_tok):
    return ref[pl.ds(c, n_tok, stride=TOKEN_ROWS), :]


def _router_kernel(h_ref, g_ref, shift_ref, scale_ref, w_ref, b_ref, t_ref, route_ref):
    t = _rms(h_ref[...]) * g_ref[...]
    t = t * (1.0 + scale_ref[0]) + shift_ref[0]
    _store_token_tiles(t_ref, t)
    t_hi, t_lo = _split_bf16(t)
    w_hi, w_lo = _split_bf16(w_ref[...])
    logits = _dot(t_hi, w_hi) + _dot(t_lo, w_hi) + _dot(t_hi, w_lo) + b_ref[...]
    lane = lax.broadcasted_iota(jnp.int32, logits.shape, 1)
    neg = -jnp.inf
    is_group = (lane >= MOE_EXPERTS) & (lane < MOE_EXPERTS + MOE_GROUPS)
    gl = jnp.where(is_group, logits, neg)
    g_max = jnp.max(gl, axis=-1, keepdims=True)
    g_idx = jnp.min(jnp.where(gl == g_max, lane, LANES), axis=-1, keepdims=True) - MOE_EXPERTS
    g_w = 1.0 / jnp.sum(jnp.exp(gl - g_max), axis=-1, keepdims=True)
    in_group = (lane >= g_idx * MOE_PER_GROUP) & (lane < (g_idx + 1) * MOE_PER_GROUP)
    el = jnp.where(in_group, logits, neg)
    m1 = jnp.max(el, axis=-1, keepdims=True)
    i1 = jnp.min(jnp.where(el == m1, lane, LANES), axis=-1, keepdims=True)
    el2 = jnp.where(lane == i1, neg, el)
    m2 = jnp.max(el2, axis=-1, keepdims=True)
    i2 = jnp.min(jnp.where(el2 == m2, lane, LANES), axis=-1, keepdims=True)
    e2 = jnp.exp(m2 - m1)
    w1 = 1.0 / (1.0 + e2)
    w2 = e2 * w1
    route = jnp.where(lane == 0, i1.astype(F32), 0.0) + jnp.where(lane == 1, i2.astype(F32), 0.0)
    route_ref[...] = route + jnp.where(lane == 2, g_w * w1, 0.0) + jnp.where(lane == 3, g_w * w2, 0.0)


def _router(h, norm_g, mods, rows, w_rt, b_rt):
    n, d = h.shape
    tm = rows.tile
    return pl.pallas_call(
        _router_kernel,
        out_shape=(jax.ShapeDtypeStruct((n * TOKEN_ROWS, LANES), F32), jax.ShapeDtypeStruct((n, LANES), F32)),
        grid=(rows.n_tiles,),
        in_specs=[pl.BlockSpec((tm, d), lambda i: (i, 0)),
                  pl.BlockSpec((1, d), lambda i: (0, 0)),
                  pl.BlockSpec((1, 1, d), lambda i: (rows.mod_index(i, 3), 0, 0)),
                  pl.BlockSpec((1, 1, d), lambda i: (rows.mod_index(i, 4), 0, 0)),
                  pl.BlockSpec((d, LANES), lambda i: (0, 0)),
                  pl.BlockSpec((1, LANES), lambda i: (0, 0))],
        out_specs=(pl.BlockSpec((tm * TOKEN_ROWS, LANES), lambda i: (i, 0)),
                   pl.BlockSpec((tm, LANES), lambda i: (i, 0))),
        compiler_params=_params("parallel"),
        name="moe_router",
    )(h, norm_g.reshape(1, d), mods, mods, w_rt, b_rt)


def _rank_kernel(route_ref, idx_ref, cnt_ref, carry_sc):
    @pl.when(pl.program_id(0) == 0)
    def _():
        carry_sc[...] = jnp.zeros(carry_sc.shape, F32)

    route = route_ref[...]
    tm = route.shape[0]
    lane = lax.broadcasted_iota(jnp.int32, route.shape, 1)
    lane_f = lane.astype(F32)
    e1, e2 = route[:, 0:1], route[:, 1:2]
    hit1, hit2 = lane_f == e1, lane_f == e2
    onehot = jnp.where(hit1 | hit2, 1.0, 0.0)
    row = lax.broadcasted_iota(jnp.int32, (tm, tm), 0)
    col = lax.broadcasted_iota(jnp.int32, (tm, tm), 1)
    earlier = jnp.where(col < row, 1.0, 0.0).astype(BF16)
    before = _dot(earlier, onehot.astype(BF16)) + carry_sc[...]
    r1 = jnp.sum(jnp.where(hit1, before, 0.0), axis=-1, keepdims=True)
    r2 = jnp.sum(jnp.where(hit2, before, 0.0), axis=-1, keepdims=True)
    carry_sc[...] += jnp.sum(onehot, axis=0, keepdims=True)
    packed = (jnp.where(lane == 0, e1, 0.0) + jnp.where(lane == 1, e2, 0.0)
              + jnp.where(lane == 2, r1, 0.0) + jnp.where(lane == 3, r2, 0.0))
    idx_ref[0] = packed.T[0:8, :].astype(jnp.int32)
    cnt_ref[...] = carry_sc[...]


def _rank(route, tm):
    n = route.shape[0]
    return pl.pallas_call(
        _rank_kernel,
        out_shape=(jax.ShapeDtypeStruct((n // tm, 8, tm), jnp.int32), jax.ShapeDtypeStruct((1, LANES), F32)),
        grid=(n // tm,),
        in_specs=[pl.BlockSpec((tm, LANES), lambda i: (i, 0))],
        out_specs=(pl.BlockSpec((1, 8, tm), lambda i: (i, 0, 0)),
                   pl.BlockSpec((1, LANES), lambda i: (0, 0))),
        scratch_shapes=[pltpu.VMEM((1, LANES), F32)],
        compiler_params=_params("arbitrary"),
        name="moe_rank",
    )(route)


ROW_DMA_UNROLL = 8


def _dispatch_kernel(dst_ref, t_ref, xs_in_ref, xs_ref, sem):
    del xs_in_ref
    tm = t_ref.shape[0] // TOKEN_ROWS

    def row_copy(r, dst):
        return pltpu.make_async_copy(t_ref.at[_token_rows(r), :], xs_ref.at[_token_rows(dst), :], sem)

    def start(i, c):
        for k in range(ROW_DMA_UNROLL):
            r = pl.multiple_of(i * ROW_DMA_UNROLL, ROW_DMA_UNROLL) + k
            row_copy(r, dst_ref[0, 0, r]).start(priority=0)
            row_copy(r, dst_ref[0, 1, r]).start(priority=1)
        return c

    def wait(i, c):
        for _ in range(2 * ROW_DMA_UNROLL):
            row_copy(0, 0).wait()
        return c

    lax.fori_loop(0, tm // ROW_DMA_UNROLL, start, 0)
    lax.fori_loop(0, tm // ROW_DMA_UNROLL, wait, 0)


def _dispatch(t, dst, xs_prev, tm):
    n = t.shape[0] // TOKEN_ROWS
    return pl.pallas_call(
        _dispatch_kernel,
        out_shape=jax.ShapeDtypeStruct(xs_prev.shape, F32),
        grid=(n // tm,),
        in_specs=[pl.BlockSpec((1, 2, tm), lambda i: (i, 0, 0), memory_space=pltpu.SMEM),
                  pl.BlockSpec((tm * TOKEN_ROWS, LANES), lambda i: (i, 0)),
                  pl.BlockSpec(memory_space=pl.ANY)],
        out_specs=pl.BlockSpec(memory_space=pl.ANY),
        scratch_shapes=[pltpu.SemaphoreType.DMA(())],
        input_output_aliases={2: 0},
        compiler_params=_params("arbitrary"),
        name="moe_dispatch",
    )(dst, t, xs_prev)


def _grouped_kernel(tile_e_ref, used_ref, x_ref, wg_ref, wu_ref, wd_ref, y_ref):
    del tile_e_ref

    @pl.when(pl.program_id(0) < used_ref[0])
    def _():
        n_tok = x_ref.shape[0] // TOKEN_ROWS
        x = jnp.concatenate([_load_token_chunk(x_ref, c, n_tok) for c in range(TOKEN_ROWS)], axis=1).astype(BF16)
        hid = _silu(_dot(x, wg_ref[0])) * _dot(x, wu_ref[0])
        _store_token_tiles(y_ref, _dot(hid.astype(BF16), wd_ref[0]))

    @pl.when(pl.program_id(0) >= used_ref[0])
    def _():
        y_ref[...] = jnp.zeros(y_ref.shape, F32)


def _grouped_experts(xs, tile_e, n_used, w_gate, w_up, w_down, tm):
    d, hid = w_gate.shape[1:]
    n_sorted = xs.shape[0] // TOKEN_ROWS
    blk = (tm * TOKEN_ROWS, LANES)
    row = lambda j, te, nu: (jnp.minimum(j, nu[0] - 1), 0)
    wsel = lambda j, te, nu: (te[j], 0, 0)
    return pl.pallas_call(
        _grouped_kernel,
        out_shape=jax.ShapeDtypeStruct(xs.shape, F32),
        grid_spec=pltpu.PrefetchScalarGridSpec(
            num_scalar_prefetch=2,
            grid=(n_sorted // tm,),
            in_specs=[pl.BlockSpec(blk, row),
                      pl.BlockSpec((1, d, hid), wsel),
                      pl.BlockSpec((1, d, hid), wsel),
                      pl.BlockSpec((1, hid, d), wsel)],
            out_specs=pl.BlockSpec(blk, lambda j, te, nu: (j, 0))),
        compiler_params=_params("arbitrary"),
        name="moe_experts",
    )(tile_e, n_used, xs, w_gate, w_up, w_down)


def _combine_kernel(dst_ref, route_ref, res_ref, gate_ref, ys_ref, o_ref, ya_sc, yb_sc, sem):
    tm = res_ref.shape[0]

    def row_copy(src, dst_sc, r):
        return pltpu.make_async_copy(ys_ref.at[_token_rows(src), :], dst_sc.at[_token_rows(r), :], sem)

    def start(i, c):
        for k in range(ROW_DMA_UNROLL):
            r = pl.multiple_of(i * ROW_DMA_UNROLL, ROW_DMA_UNROLL) + k
            row_copy(dst_ref[0, 0, r], ya_sc, r).start(priority=0)
            row_copy(dst_ref[0, 1, r], yb_sc, r).start(priority=1)
        return c

    def wait(i, c):
        for _ in range(2 * ROW_DMA_UNROLL):
            row_copy(0, ya_sc, 0).wait()
        return c

    lax.fori_loop(0, tm // ROW_DMA_UNROLL, start, 0)
    lax.fori_loop(0, tm // ROW_DMA_UNROLL, wait, 0)
    route = route_ref[...]
    w1, w2 = route[:, 2:3], route[:, 3:4]
    for c in range(TOKEN_ROWS):
        cols = slice(c * LANES, (c + 1) * LANES)
        mix = w1 * _load_token_chunk(ya_sc, c, tm) + w2 * _load_token_chunk(yb_sc, c, tm)
        o_ref[:, cols] = res_ref[:, cols] + gate_ref[0, :, cols] * mix


def _combine(ys, dst, route, res, mods, rows):
    n, d = res.shape
    tm = rows.tile
    return pl.pallas_call(
        _combine_kernel,
        out_shape=jax.ShapeDtypeStruct((n, d), F32),
        grid=(rows.n_tiles,),
        in_specs=[pl.BlockSpec((1, 2, tm), lambda i: (i, 0, 0), memory_space=pltpu.SMEM),
                  pl.BlockSpec((tm, LANES), lambda i: (i, 0)),
                  pl.BlockSpec((tm, d), lambda i: (i, 0)),
                  pl.BlockSpec((1, 1, d), lambda i: (rows.mod_index(i, 5), 0, 0)),
                  pl.BlockSpec(memory_space=pl.ANY)],
        out_specs=pl.BlockSpec((tm, d), lambda i: (i, 0)),
        scratch_shapes=[pltpu.VMEM((tm * TOKEN_ROWS, LANES), F32), pltpu.VMEM((tm * TOKEN_ROWS, LANES), F32),
                        pltpu.SemaphoreType.DMA(())],
        compiler_params=_params("arbitrary"),
        name="moe_combine",
    )(dst, route, res, mods, ys)


def _moe_layer(h, mods, rows, xs_prev, norm_g, w_group, b_group, w_router, b_router, w_gate, w_up, w_down):
    n, d = h.shape
    pad = LANES - MOE_EXPERTS - MOE_GROUPS
    w_rt = jnp.concatenate([w_router, w_group, jnp.zeros((d, pad), F32)], axis=1)
    b_rt = jnp.concatenate([b_router, b_group, jnp.zeros((pad,), F32)]).reshape(1, LANES)
    t, route = _router(h, norm_g, mods, rows, w_rt, b_rt)
    idx, counts = _rank(route, rows.tile)

    tile = MOE_TILE
    max_tiles = xs_prev.shape[0] // (tile * TOKEN_ROWS)
    cnt = counts[0, :MOE_EXPERTS].astype(jnp.int32)
    padded = (cnt + tile - 1) // tile * tile
    ends = jnp.cumsum(padded)
    offs = ends - padded
    n_used = (ends[-1] // tile).reshape(1)
    tile_start = jnp.minimum(jnp.arange(max_tiles, dtype=jnp.int32), n_used - 1) * tile
    tile_e = jnp.sum((ends[None, :] <= tile_start[:, None]).astype(jnp.int32), axis=1)
    tile_e = jnp.minimum(tile_e, MOE_EXPERTS - 1)
    dst = idx[:, 2:4, :] + offs[idx[:, 0:2, :]]

    xs = _dispatch(t, dst, xs_prev, rows.tile)
    ys = _grouped_experts(xs, tile_e, n_used, w_gate.astype(BF16), w_up.astype(BF16), w_down.astype(BF16), tile)
    return _combine(ys, dst, route, h, mods, rows), xs


def _scan_block(step, n_ctx_blocks, n_blocks, rev):
    if not rev:
        return step
    return jnp.where(step < n_ctx_blocks, n_ctx_blocks - 1 - step, 2 * n_ctx_blocks + n_blocks - 1 - step)


def _s5_disc_kernel(a_re_ref, a_im_ref, ldt_ref, b_re_ref, b_im_ref, abr_ref, abi_ref, bbr_ref, bbi_ref):
    a_re, a_im = a_re_ref[...], a_im_ref[...]
    dt = jnp.exp(ldt_ref[...])
    mag = jnp.exp(a_re * dt)
    ab_re = mag * jnp.cos(a_im * dt)
    ab_im = mag * jnp.sin(a_im * dt)
    den = a_re * a_re + a_im * a_im
    num_re = ab_re - 1.0
    coef_re = (num_re * a_re + ab_im * a_im) / den
    coef_im = (ab_im * a_re - num_re * a_im) / den
    abr_ref[...] = ab_re
    abi_ref[...] = ab_im
    for c in range(S5_GROUP):
        b_re, b_im = b_re_ref[c], b_im_ref[c]
        bbr_ref[c] = coef_re * b_re - coef_im * b_im
        bbi_ref[c] = coef_re * b_im + coef_im * b_re


def _s5_discretise(a_re, a_im, log_dt, b_re, b_im):
    nd, g, p = a_re.shape
    n = nd * g
    bt = lambda b: jnp.transpose(b, (3, 0, 1, 2)).reshape(S5_GROUP, n, p)
    shp = jax.ShapeDtypeStruct((n, p), F32)
    shp_b = jax.ShapeDtypeStruct((S5_GROUP, n, p), F32)
    return pl.pallas_call(_s5_disc_kernel, out_shape=(shp, shp, shp_b, shp_b), name="s5_discretise")(
        a_re.reshape(n, p), a_im.reshape(n, p), log_dt.reshape(n, 1), bt(b_re), bt(b_im))


def _s5_scan_kernel(u_ref, bm_ref, cm_ref, ar_ref, ai_ref, y_ref, bu_sc, st_sc, *, nb, steps, rev):
    @pl.when(pl.program_id(0) == 0)
    def _():
        st_sc[...] = jnp.zeros(st_sc.shape, F32)

    n_slabs = u_ref.shape[1] // LANES
    ns = S5_SLAB_STATE
    for s in range(n_slabs):
        us = u_ref[:, s * LANES:(s + 1) * LANES].astype(BF16)
        bu_sc[...] = _dot(us, bm_ref[s])
        ar = jnp.broadcast_to(ar_ref[s], (nb, ns))
        ai = jnp.broadcast_to(ai_ref[s], (nb, ns))

        def step(k, carry):
            xr, xi = carry
            t = (steps - 1 - k) if rev else k
            r0 = pl.multiple_of(t * nb, nb)
            br = bu_sc[pl.ds(r0, nb), 0:ns]
            bi = bu_sc[pl.ds(r0, nb), ns:2 * ns]
            nr = ar * xr - ai * xi + br
            ni = ar * xi + ai * xr + bi
            bu_sc[pl.ds(r0, nb), 0:ns] = nr
            bu_sc[pl.ds(r0, nb), ns:2 * ns] = ni
            return nr, ni

        xr, xi = lax.fori_loop(0, steps, step, (st_sc[s, 0], st_sc[s, 1]), unroll=8)
        st_sc[s, 0] = xr
        st_sc[s, 1] = xi
        y_ref[:, s * LANES:(s + 1) * LANES] = _dot(bu_sc[...].astype(BF16), cm_ref[s])


def _s5_scan(u_tm, bmat, cmat, ab_re, ab_im, nb, n_ctx, rev):
    n, d = u_tm.shape
    steps = S5_STEPS
    blk = steps * nb
    n_blocks = n // blk
    n_ctx_blocks = n_ctx // steps
    n_slabs = d // LANES
    ns = S5_SLAB_STATE
    idx = lambda s: (_scan_block(s, n_ctx_blocks, n_blocks - n_ctx_blocks, rev), 0)
    return pl.pallas_call(
        functools.partial(_s5_scan_kernel, nb=nb, steps=steps, rev=rev),
        out_shape=jax.ShapeDtypeStruct((n, d), F32),
        grid=(n_blocks,),
        in_specs=[pl.BlockSpec((blk, d), idx),
                  pl.BlockSpec((n_slabs, LANES, 2 * ns), lambda s: (0, 0, 0)),
                  pl.BlockSpec((n_slabs, 2 * ns, LANES), lambda s: (0, 0, 0)),
                  pl.BlockSpec((n_slabs, 1, ns), lambda s: (0, 0, 0)),
                  pl.BlockSpec((n_slabs, 1, ns), lambda s: (0, 0, 0))],
        out_specs=pl.BlockSpec((blk, d), idx),
        scratch_shapes=[pltpu.VMEM((blk, 2 * ns), F32),
                        pltpu.VMEM((n_slabs, 2, nb, ns), F32)],
        compiler_params=_params("arbitrary"),
        name="s5_scan_bwd" if rev else "s5_scan_fwd",
    )(u_tm, bmat, cmat, ab_re, ab_im)


def _gelu_tanh(x):
    return 0.5 * x * (1.0 + jnp.tanh(math.sqrt(2.0 / math.pi) * (x + 0.044715 * (x * x * x))))


def _s5_glu_kernel(yf_ref, yb_ref, u_ref, d_ref, wa_ref, wb_ref, ba_ref, bb_ref, o_ref):
    y = yf_ref[...] + yb_ref[...] + u_ref[...] * d_ref[...]
    z = _gelu_tanh(y).astype(BF16)
    a = _dot(z, wa_ref[...]) + ba_ref[...]
    b = _dot(z, wb_ref[...]) + bb_ref[...]
    o_ref[...] = a * _sigmoid(b)


def _s5_glu(y_f, y_b, u, d_skip, w_glu, b_glu, tm):
    n, d = u.shape
    row = pl.BlockSpec((tm, d), lambda i: (i, 0))
    vec = pl.BlockSpec((1, d), lambda i: (0, 0))
    return pl.pallas_call(
        _s5_glu_kernel,
        out_shape=jax.ShapeDtypeStruct((n, d), F32),
        grid=(n // tm,),
        in_specs=[row, row, row, vec,
                  pl.BlockSpec((d, d), lambda i: (0, 0)),
                  pl.BlockSpec((d, d), lambda i: (0, 1)),
                  vec,
                  pl.BlockSpec((1, d), lambda i: (0, 1))],
        out_specs=row,
        compiler_params=_params("parallel"),
        name="s5_glu",
    )(y_f, y_b, u, d_skip.reshape(1, d), w_glu, w_glu, b_glu.reshape(1, 2 * d), b_glu.reshape(1, 2 * d))


def _residual_kernel(h_ref, y_ref, gate_ref, o_ref):
    o_ref[...] = h_ref[...] + gate_ref[0] * y_ref[...]


def _residual(h, y, mods, rows, gate_row):
    n, d = h.shape
    tm = rows.tile
    row = pl.BlockSpec((tm, d), lambda i: (i, 0))
    return pl.pallas_call(
        _residual_kernel,
        out_shape=jax.ShapeDtypeStruct((n, d), F32),
        grid=(rows.n_tiles,),
        in_specs=[row, row, pl.BlockSpec((1, 1, d), lambda i: (rows.mod_index(i, gate_row), 0, 0))],
        out_specs=row,
        compiler_params=_params("parallel"),
        name="residual",
    )(h, y, mods)


def _s5_layer(h, mods, rows, norm_g, a_re, a_im, log_dt, b_re, b_im, c_re, c_im, d_skip, w_glu, b_glu):
    nb, seq = rows.batch, rows.seq
    d = h.shape[1]
    n_slabs = d // LANES
    sg = S5_SLAB_GROUPS
    ab_re, ab_im, bb_re, bb_im = _s5_discretise(a_re, a_im, log_dt, b_re, b_im)
    eye = jnp.eye(sg, dtype=F32)

    def b_blocks(bb):
        t = bb.reshape(S5_GROUP, 2, n_slabs, sg, S5_STATE)
        t = jnp.einsum("cdsgp,gh->dsgchp", t, eye)
        return t.reshape(2, n_slabs, LANES, S5_SLAB_STATE)

    def c_blocks(cc):
        t = cc.reshape(2, n_slabs, sg, S5_GROUP, S5_STATE)
        t = jnp.einsum("dsgcp,gh->dshpgc", t, eye)
        return t.reshape(2, n_slabs, S5_SLAB_STATE, LANES)

    bmat = jnp.concatenate([b_blocks(bb_re), b_blocks(bb_im)], axis=-1).astype(BF16)
    cmat = jnp.concatenate([c_blocks(c_re), -c_blocks(c_im)], axis=2).astype(BF16)
    ab_re = ab_re.reshape(2, n_slabs, 1, S5_SLAB_STATE)
    ab_im = ab_im.reshape(2, n_slabs, 1, S5_SLAB_STATE)

    u = _norm_mod(h, norm_g, mods, rows, 0, F32)
    u_tm = jnp.transpose(u.reshape(nb, seq, d), (1, 0, 2)).reshape(seq * nb, d)
    y_f = _s5_scan(u_tm, bmat[0], cmat[0], ab_re[0], ab_im[0], nb, rows.ctx, False)
    y_b = _s5_scan(u_tm, bmat[1], cmat[1], ab_re[1], ab_im[1], nb, rows.ctx, True)
    mix_tm = _s5_glu(y_f, y_b, u_tm, d_skip, w_glu.astype(BF16), b_glu, S5_STEPS * nb)
    mix = jnp.transpose(mix_tm.reshape(seq, nb, d), (1, 0, 2)).reshape(nb * seq, d)
    return _residual(h, mix, mods, rows, 2)


def _ssd_conv_kernel(x_ref, w_ref, b_ref, o_ref, pad_sc, *, ctx, n_lat, chunk):
    tc = x_ref.shape[2]
    zeros = jnp.zeros((8, tc), F32)
    lat0 = 16 + ctx
    pad_sc[0:8, :] = zeros
    pad_sc[8 + ctx:lat0, :] = zeros
    pad_sc[lat0 + n_lat:lat0 + n_lat + 8, :] = zeros
    half = SSD_CONV // 2
    segments = ((0, ctx, 8), (ctx, n_lat, lat0))
    for src0, length, dst0 in segments:
        for r in range(0, length, chunk):
            pad_sc[dst0 + r:dst0 + r + chunk, :] = x_ref[0, src0 + r:src0 + r + chunk, :]
    w = w_ref[...]
    for src0, length, dst0 in segments:
        for r in range(0, length, chunk):
            acc = jnp.broadcast_to(b_ref[...], (chunk, tc))
            for k in range(SSD_CONV):
                lo = dst0 + r + k - half
                acc = acc + w[k:k + 1, :] * pad_sc[lo:lo + chunk, :]
            o_ref[0, src0 + r:src0 + r + chunk, :] = _silu(acc)


def _ssd_conv(xbc, conv_w, conv_b, ctx):
    nb, seq, ch = xbc.shape
    tc = LANES
    n_lat = seq - ctx
    return pl.pallas_call(
        functools.partial(_ssd_conv_kernel, ctx=ctx, n_lat=n_lat, chunk=ROW_TILE),
        out_shape=jax.ShapeDtypeStruct((nb, seq, ch), F32),
        grid=(nb, ch // tc),
        in_specs=[pl.BlockSpec((1, seq, tc), lambda b, j: (b, 0, j)),
                  pl.BlockSpec((SSD_CONV, tc), lambda b, j: (0, j)),
                  pl.BlockSpec((1, tc), lambda b, j: (0, j))],
        out_specs=pl.BlockSpec((1, seq, tc), lambda b, j: (b, 0, j)),
        scratch_shapes=[pltpu.VMEM((seq + 24, tc), F32)],
        compiler_params=_params("parallel", "parallel"),
        name="ssd_conv",
    )(xbc, conv_w, conv_b.reshape(1, ch))


def _softplus(x):
    return jnp.maximum(x, 0.0) + jnp.log(1.0 + jnp.exp(-jnp.abs(x)))


def _ssd_scan_kernel(xbc_ref, dt_ref, dtb_ref, a_ref, y_ref, st_sc, *, rev, col0, n_heads):
    @pl.when(pl.program_id(1) == 0)
    def _():
        st_sc[...] = jnp.zeros(st_sc.shape, F32)

    q = xbc_ref.shape[1]
    hpg = n_heads // SSD_GROUPS
    inner = n_heads * SSD_HEAD_DIM
    dt = _softplus(dt_ref[0] + dtb_ref[...])
    a = dt * a_ref[...]
    row = lax.broadcasted_iota(jnp.int32, (q, q), 0)
    col = lax.broadcasted_iota(jnp.int32, (q, q), 1)
    tri = (col >= row) if rev else (col <= row)
    tri_b = jnp.where(tri, 1.0, 0.0).astype(BF16)
    a_hi, a_lo = _split_bf16(a)
    cum = _dot(tri_b, a_hi) + _dot(tri_b, a_lo)
    cum_t = cum.T
    total = cum[0:1, :] if rev else cum[q - 1:q, :]
    for g in range(SSD_GROUPS):
        bg = xbc_ref[0, :, inner + g * SSD_STATE:inner + (g + 1) * SSD_STATE]
        cg = xbc_ref[0, :, inner + (SSD_GROUPS + g) * SSD_STATE:inner + (SSD_GROUPS + g + 1) * SSD_STATE]
        cg_b = cg.astype(BF16)
        cb = lax.dot_general(cg_b, bg.astype(BF16), (((1,), (1,)), ((), ())), preferred_element_type=F32)
        bg_t = bg.T
        for hh in range(hpg):
            hd = g * hpg + hh
            c = col0 + hd
            cum_c = cum[:, c:c + 1]
            cum_r = cum_t[c:c + 1, :]
            decay = jnp.exp(jnp.where(tri, cum_c - cum_r, -jnp.inf))
            xdt = (xbc_ref[0, :, hd * SSD_HEAD_DIM:(hd + 1) * SSD_HEAD_DIM] * dt[:, c:c + 1]).astype(BF16)
            state = st_sc[hd]
            y = _dot((cb * decay).astype(BF16), xdt) + jnp.exp(cum_c) * _dot(cg_b, state.astype(BF16))
            y_ref[0, :, hd * SSD_HEAD_DIM:(hd + 1) * SSD_HEAD_DIM] = y
            tot = total[:, c:c + 1]
            tail = jnp.exp(tot - cum_r)
            st_sc[hd] = jnp.exp(tot) * state + _dot((bg_t * tail).astype(BF16), xdt)


def _ssd_scan(xbc, dt_raw, dt_bias, a_neg, ctx, rev, n_heads):
    nb, seq, ch = xbc.shape
    q = SSD_CHUNK
    n_blocks = seq // q
    n_ctx_blocks = ctx // q
    inner = n_heads * SSD_HEAD_DIM
    blk = lambda b, s: (b, _scan_block(s, n_ctx_blocks, n_blocks - n_ctx_blocks, rev), 0)
    return pl.pallas_call(
        functools.partial(_ssd_scan_kernel, rev=rev, col0=n_heads if rev else 0, n_heads=n_heads),
        out_shape=jax.ShapeDtypeStruct((nb, seq, inner), F32),
        grid=(nb, n_blocks),
        in_specs=[pl.BlockSpec((1, q, ch), blk),
                  pl.BlockSpec((1, q, LANES), blk),
                  pl.BlockSpec((1, LANES), lambda b, s: (0, 0)),
                  pl.BlockSpec((1, LANES), lambda b, s: (0, 0))],
        out_specs=pl.BlockSpec((1, q, inner), blk),
        scratch_shapes=[pltpu.VMEM((n_heads, SSD_STATE, SSD_HEAD_DIM), F32)],
        compiler_params=_params("parallel", "arbitrary"),
        name="ssd_scan_bwd" if rev else "ssd_scan_fwd",
    )(xbc, dt_raw, dt_bias, a_neg)


def _ssd_finish_kernel(yf_ref, yb_ref, x_ref, z_ref, d_ref, g_ref, o_ref):
    y = yf_ref[...] + yb_ref[...] + x_ref[...] * d_ref[...]
    o_ref[...] = (_rms(y * _silu(z_ref[...])) * g_ref[...]).astype(o_ref.dtype)


def _ssd_finish(y_f, y_b, xbc, z, d_skip, norm_g, tm):
    n, inner = z.shape
    row = pl.BlockSpec((tm, inner), lambda i: (i, 0))
    vec = pl.BlockSpec((1, inner), lambda i: (0, 0))
    return pl.pallas_call(
        _ssd_finish_kernel,
        out_shape=jax.ShapeDtypeStruct((n, inner), BF16),
        grid=(n // tm,),
        in_specs=[row, row, row, row, vec, vec],
        out_specs=row,
        compiler_params=_params("parallel"),
        name="ssd_finish",
    )(y_f, y_b, xbc, z, d_skip, norm_g)


def _ssd_layer(h, mods, rows, norm_g, w_in, conv_w, conv_b, dt_bias, a_log, d_skip, ssd_norm_g, w_out):
    nb, seq = rows.batch, rows.seq
    n = nb * seq
    n_heads = dt_bias.shape[1]
    inner = n_heads * SSD_HEAD_DIM
    conv_ch = inner + 2 * SSD_GROUPS * SSD_STATE
    u = _norm_mod(h, norm_g, mods, rows, 0, BF16)
    w_in = w_in.astype(BF16)
    d = w_in.shape[0]
    z = _matmul(u, w_in[:, :inner], F32, tn=1024)
    xbc = _matmul(u, w_in[:, inner:inner + conv_ch], F32, tn=1024)
    w_dt = jnp.concatenate([w_in[:, inner + conv_ch:], jnp.zeros((d, LANES - 2 * n_heads), BF16)], axis=1)
    dt_raw = _matmul(u, w_dt, F32)
    xbc = _ssd_conv(xbc.reshape(nb, seq, conv_ch), conv_w, conv_b, rows.ctx)
    pad = jnp.zeros((LANES - 2 * n_heads,), F32)
    dtb = jnp.concatenate([dt_bias.reshape(-1), pad]).reshape(1, LANES)
    a_neg = jnp.concatenate([a_log.reshape(-1), pad]).reshape(1, LANES)
    a_neg = _neg_exp(a_neg, 2 * n_heads)
    dt3 = dt_raw.reshape(nb, seq, LANES)
    y_f = _ssd_scan(xbc, dt3, dtb, a_neg, rows.ctx, False, n_heads)
    y_b = _ssd_scan(xbc, dt3, dtb, a_neg, rows.ctx, True, n_heads)
    dsk = jnp.repeat(d_skip, SSD_HEAD_DIM).reshape(1, inner)
    g = _ssd_finish(y_f.reshape(n, inner), y_b.reshape(n, inner), xbc.reshape(n, conv_ch), z, dsk,
                    ssd_norm_g.reshape(1, inner), rows.tile)
    return _matmul_residual(g, w_out.astype(BF16), h, mods, rows, 2)


def _neg_exp_kernel(x_ref, o_ref, *, valid):
    lane = lax.broadcasted_iota(jnp.int32, x_ref.shape, 1)
    o_ref[...] = jnp.where(lane < valid, -jnp.exp(x_ref[...]), 0.0)


def _neg_exp(x, valid):
    return pl.pallas_call(functools.partial(_neg_exp_kernel, valid=valid),
                          out_shape=jax.ShapeDtypeStruct(x.shape, F32), name="ssd_decay_rate")(x)


def kernel(x, c, ctx, c_ctx, mod_w, mod_b, norm1_g, norm2_g, attn_w_qkv, attn_q_gain, attn_k_gain, attn_w_o, s5_a_re, s5_a_im, s5_log_dt, s5_b_re, s5_b_im, s5_c_re, s5_c_im, s5_d, s5_w_glu, s5_b_glu, ssd_w_in, ssd_conv_w, ssd_conv_b, ssd_dt_bias, ssd_a_log, ssd_d, ssd_norm_g, ssd_w_out, moe_w_group, moe_b_group, moe_w_router, moe_b_router, moe_w_gate, moe_w_up, moe_w_down):
    batch, n_lat, d = x.shape
    n_ctx = ctx.shape[1]
    depth = mod_w.shape[0]
    seq = n_ctx + n_lat
    rows = _Rows(batch, seq, n_ctx, ROW_TILE)

    n_cond = -(-(batch + 1) // 8) * 8
    cond = jnp.concatenate([c, c_ctx[None, :], jnp.zeros((n_cond - batch - 1, d), F32)], axis=0)
    mods_all = _modulation(cond, mod_w, mod_b).reshape(depth, n_cond * 6, 1, d)

    cos, sin = _rope_tables(n_lat, n_ctx)
    h = jnp.concatenate([ctx, x], axis=1).reshape(batch * seq, d)

    max_tiles = -(-2 * batch * seq // MOE_TILE) + MOE_EXPERTS
    xs_buf = jnp.zeros((max_tiles * MOE_TILE * TOKEN_ROWS, LANES), F32)
    for i in range(depth):
        need_ctx = i < depth - 1
        kind, j = i % N_MIXERS, i // N_MIXERS
        mods = mods_all[i]
        if kind == 0:
            h = _attention_layer(h, mods, rows, cos, sin, norm1_g[i], attn_w_qkv[j], attn_q_gain[j],
                                 attn_k_gain[j], attn_w_o[j], need_ctx)
        elif kind == 1:
            h = _s5_layer(h, mods, rows, norm1_g[i], s5_a_re[j], s5_a_im[j], s5_log_dt[j], s5_b_re[j], s5_b_im[j],
                          s5_c_re[j], s5_c_im[j], s5_d[j], s5_w_glu[j], s5_b_glu[j])
        else:
            h = _ssd_layer(h, mods, rows, norm1_g[i], ssd_w_in[j], ssd_conv_w[j], ssd_conv_b[j], ssd_dt_bias[j],
                           ssd_a_log[j], ssd_d[j], ssd_norm_g[j], ssd_w_out[j])
        h, xs_buf = _moe_layer(h, mods, rows, xs_buf, norm2_g[i], moe_w_group[i], moe_b_group[i], moe_w_router[i],
                               moe_b_router[i], moe_w_gate[i], moe_w_up[i], moe_w_down[i])
    return h.reshape(batch, seq, d)[:, n_ctx:, :]
```
